```python
import jax, jax.numpy as jnp
from jax import lax
import numpy as np

D_MODEL = 1024
BATCH = 4
SEQ = 8192
DEPTH = 1

PLE_DIM = 256
ROPE_THETA = 10000.0
EPS = 1e-6
NEG_INF = -1e30

MOBA_HEADS = 8
MOBA_HEAD_DIM = 64
MOBA_WIDTH = MOBA_HEADS * MOBA_HEAD_DIM
MOBA_BLOCK = 256
MOBA_TOPK = 3
MOBA_Q_CHUNK = 64

MLA_HEADS = 8
MLA_Q_RANK = 256
MLA_KV_RANK = 128
MLA_NOPE_DIM = 64
MLA_ROPE_DIM = 32
MLA_V_DIM = 64
MLA_QK_DIM = MLA_NOPE_DIM + MLA_ROPE_DIM
MLA_WIDTH = MLA_HEADS * MLA_V_DIM
MLA_Q_CHUNK = 128

D_FF = -(-8 * D_MODEL // (3 * 256)) * 256

IN_SIZES = (MOBA_WIDTH, MOBA_WIDTH, MOBA_WIDTH,
            MLA_Q_RANK, MLA_KV_RANK, MLA_ROPE_DIM,
            D_MODEL, D_MODEL)
IN_COLS = sum(IN_SIZES)

kernel_name = "hybrid_moba_mla_gated_block"


def _split_cols(t, sizes):
    offs = np.cumsum(sizes)[:-1].tolist()
    return jnp.split(t, offs, axis=-1)


def rmsnorm(x, g):
    xf = x.astype(jnp.float32)
    xf = xf * lax.rsqrt(jnp.mean(xf * xf, axis=-1, keepdims=True) + EPS)
    return (xf * g.astype(jnp.float32)).astype(x.dtype)


def rope(x, pos):
    d = x.shape[-1]
    half = d // 2
    inv_freq = 1.0 / (ROPE_THETA ** (jnp.arange(half, dtype=jnp.float32) * (2.0 / d)))
    ang = pos.astype(jnp.float32)[:, None] * inv_freq[None, :]
    cos = jnp.cos(ang).astype(x.dtype)
    sin = jnp.sin(ang).astype(x.dtype)
    x1, x2 = x[..., :half], x[..., half:]
    return jnp.concatenate([x1 * cos - x2 * sin, x2 * cos + x1 * sin], axis=-1)


def split_heads(t, n_heads):
    b, s, w = t.shape
    return t.reshape(b, s, n_heads, w // n_heads).transpose(0, 2, 1, 3)


def merge_heads(t):
    b, h, s, d = t.shape
    return t.transpose(0, 2, 1, 3).reshape(b, s, h * d)


def moba_attention(q, k, v):
    B, H, S, Dh = q.shape
    nb = -(-S // MOBA_BLOCK)
    pad = nb * MOBA_BLOCK - S
    kp = jnp.pad(k, ((0, 0), (0, 0), (0, pad), (0, 0)))
    vp = jnp.pad(v, ((0, 0), (0, 0), (0, pad), (0, 0)))
    kb = kp.reshape(B, H, nb, MOBA_BLOCK, Dh)
    vb = vp.reshape(B, H, nb, MOBA_BLOCK, Dh)
    k_mean = jnp.mean(kb.astype(jnp.float32), axis=3)
    k_sel_n = min(MOBA_TOPK, nb)
    scale = Dh ** -0.5
    n_chunks = S // MOBA_Q_CHUNK
    gather_blocks = jax.vmap(jax.vmap(lambda blocks, idx: blocks[idx]))

    def chunk(c):
        q0 = c * MOBA_Q_CHUNK
        qc = lax.dynamic_slice_in_dim(q, q0, MOBA_Q_CHUNK, axis=2)
        qpos = q0 + jnp.arange(MOBA_Q_CHUNK)
        cur = q0 // MOBA_BLOCK
        gate = jnp.einsum('bhqd,bhnd->bhqn', qc.astype(jnp.float32), k_mean)
        past = jnp.arange(nb) < cur
        gate = jnp.where(past[None, None, None, :], gate, NEG_INF)
        gval, idx = lax.top_k(gate, k_sel_n)
        valid = gval > (NEG_INF * 0.5)
        k_sel = gather_blocks(kb, idx)
        v_sel = gather_blocks(vb, idx)
        s_sel = jnp.einsum('bhqd,bhqkjd->bhqkj', qc, k_sel).astype(jnp.float32) * scale
        s_sel = jnp.where(valid[..., None], s_sel, NEG_INF)
        s_sel = s_sel.reshape(B, H, MOBA_Q_CHUNK, k_sel_n * MOBA_BLOCK)
        k_own = lax.dynamic_index_in_dim(kb, cur, axis=2, keepdims=False)
        v_own = lax.dynamic_index_in_dim(vb, cur, axis=2, keepdims=False)
        kpos = cur * MOBA_BLOCK + jnp.arange(MOBA_BLOCK)
        s_own = jnp.einsum('bhqd,bhjd->bhqj', qc, k_own).astype(jnp.float32) * scale
        s_own = jnp.where((kpos[None, :] <= qpos[:, None])[None, None], s_own, NEG_INF)
        probs = jax.nn.softmax(jnp.concatenate([s_sel, s_own], axis=-1), axis=-1).astype(v.dtype)
        p_sel = probs[..., :k_sel_n * MOBA_BLOCK].reshape(B, H, MOBA_Q_CHUNK, k_sel_n, MOBA_BLOCK)
        p_own = probs[..., k_sel_n * MOBA_BLOCK:]
        return (jnp.einsum('bhqkj,bhqkjd->bhqd', p_sel, v_sel)
                + jnp.einsum('bhqj,bhjd->bhqd', p_own, v_own))

    out = lax.map(chunk, jnp.arange(n_chunks))
    return jnp.moveaxis(out, 0, 2).reshape(B, H, S, Dh)


def mla_attention(q_nope, q_rope, k_nope, k_rope, v):
    B, H, S, _ = q_nope.shape
    scale = MLA_QK_DIM ** -0.5
    kpos = jnp.arange(S)
    n_chunks = S // MLA_Q_CHUNK

    def chunk(c):
        q0 = c * MLA_Q_CHUNK
        qn = lax.dynamic_slice_in_dim(q_nope, q0, MLA_Q_CHUNK, axis=2)
        qr = lax.dynamic_slice_in_dim(q_rope, q0, MLA_Q_CHUNK, axis=2)
        s = (jnp.einsum('bhqd,bhkd->bhqk', qn, k_nope)
             + jnp.einsum('bhqd,bkd->bhqk', qr, k_rope)).astype(jnp.float32) * scale
        qpos = q0 + jnp.arange(MLA_Q_CHUNK)
        s = jnp.where((kpos[None, :] <= qpos[:, None])[None, None], s, NEG_INF)
        probs = jax.nn.softmax(s, axis=-1).astype(v.dtype)
        return jnp.einsum('bhqk,bhkd->bhqd', probs, v)

    out = lax.map(chunk, jnp.arange(n_chunks))
    return jnp.moveaxis(out, 0, 2).reshape(B, H, S, MLA_V_DIM)


def setup_inputs(seed: int = 0) -> dict:
    key = jax.random.key(seed)
    ks = jax.random.split(key, 20)

    def w(k, shape, fan_in):
        return jax.random.normal(k, shape, jnp.float32) * (fan_in ** -0.5)

    def gain(k, shape):
        return 1.0 + 0.05 * jax.random.normal(k, shape, jnp.float32)

    return {
        "x": jax.random.normal(ks[0], (BATCH, SEQ, D_MODEL), jnp.float32),
        "p": jax.random.normal(ks[1], (DEPTH, BATCH, SEQ, PLE_DIM), jnp.float32),
        "attn_norm": gain(ks[2], (DEPTH, D_MODEL)),
        "w_in": w(ks[3], (DEPTH, D_MODEL, IN_COLS), D_MODEL),
        "mla_q_norm": gain(ks[4], (DEPTH, MLA_Q_RANK)),
        "w_q_b": w(ks[5], (DEPTH, MLA_Q_RANK, MLA_HEADS * MLA_QK_DIM), MLA_Q_RANK),
        "mla_kv_norm": gain(ks[6], (DEPTH, MLA_KV_RANK)),
        "w_kv_b": w(ks[7], (DEPTH, MLA_KV_RANK, MLA_HEADS * (MLA_NOPE_DIM + MLA_V_DIM)), MLA_KV_RANK),
        "w_moba_branch": w(ks[8], (DEPTH, MOBA_WIDTH, D_MODEL), MOBA_WIDTH),
        "w_mla_branch": w(ks[9], (DEPTH, MLA_WIDTH, D_MODEL), MLA_WIDTH),
        "w_o": w(ks[10], (DEPTH, D_MODEL, D_MODEL), D_MODEL),
        "ffn_norm": gain(ks[11], (DEPTH, D_MODEL)),
        "w_gate_up": w(ks[12], (DEPTH, D_MODEL, 2 * D_FF), D_MODEL),
        "w_down": w(ks[13], (DEPTH, D_FF, D_MODEL), D_FF),
        "ple_norm": gain(ks[14], (DEPTH, D_MODEL)),
        "w_ple_gate": w(ks[15], (DEPTH, D_MODEL, D_MODEL), D_MODEL),
        "w_ple_proj": w(ks[16], (DEPTH, PLE_DIM, D_MODEL), PLE_DIM),
        "final_norm": gain(ks[17], (D_MODEL,)),
    }


def reference(x, p, attn_norm, w_in, mla_q_norm, w_q_b, mla_kv_norm, w_kv_b,
              w_moba_branch, w_mla_branch, w_o, ffn_norm, w_gate_up, w_down,
              ple_norm, w_ple_gate, w_ple_proj, final_norm):
    B, S, _ = x.shape
    pos = jnp.arange(S)
    for i in range(DEPTH):
        h = rmsnorm(x, attn_norm[i])
        proj = h @ w_in[i]
        qa, ka, va, cq, ckv, kr, ga, gb = _split_cols(proj, IN_SIZES)

        qa = rope(split_heads(qa, MOBA_HEADS), pos)
        ka = rope(split_heads(ka, MOBA_HEADS), pos)
        va = split_heads(va, MOBA_HEADS)
        ya = merge_heads(moba_attention(qa, ka, va)) @ w_moba_branch[i]

        qm = split_heads(rmsnorm(cq, mla_q_norm[i]) @ w_q_b[i], MLA_HEADS)
        q_nope, q_rope = qm[..., :MLA_NOPE_DIM], rope(qm[..., MLA_NOPE_DIM:], pos)
        kvm = split_heads(rmsnorm(ckv, mla_kv_norm[i]) @ w_kv_b[i], MLA_HEADS)
        k_nope, vm = kvm[..., :MLA_NOPE_DIM], kvm[..., MLA_NOPE_DIM:]
        k_rope = rope(kr, pos)
        yb = merge_heads(mla_attention(q_nope, q_rope, k_nope, k_rope, vm)) @ w_mla_branch[i]

        mix = jax.nn.sigmoid(ga) * ya + jax.nn.sigmoid(gb) * yb
        x = x + mix @ w_o[i]

        h = rmsnorm(x, ffn_norm[i])
        g, u = jnp.split(h @ w_gate_up[i], 2, axis=-1)
        x = x + (jax.nn.silu(g) * u) @ w_down[i]

        h = rmsnorm(x, ple_norm[i])
        x = x + jax.nn.sigmoid(h @ w_ple_gate[i]) * (p[i] @ w_ple_proj[i])
    return rmsnorm(x, final_norm)
```

```python
import functools

import jax
import jax.numpy as jnp
import numpy as np
from jax import lax
from jax.experimental import pallas as pl
from jax.experimental.pallas import tpu as pltpu

F32 = jnp.float32
BF16 = jnp.bfloat16

D_MODEL = 1024
PLE_DIM = 256
ROPE_THETA = 10000.0
EPS = 1e-6
NEG_INF = -1e30

HEADS = 8
HEAD_DIM = 64
MOBA_WIDTH = HEADS * HEAD_DIM
MOBA_BLOCK = 256
MOBA_TOPK = 3
MLA_Q_RANK = 256
MLA_KV_RANK = 128
MLA_ROPE_DIM = 32
MLA_QK_DIM = HEAD_DIM + MLA_ROPE_DIM
D_FF = 2816

LANES = 128
AUX_LANE = HEAD_DIM
FF_CHUNK = 256
N_FF_CHUNKS = D_FF // FF_CHUNK
assert N_FF_CHUNKS * FF_CHUNK == D_FF

PROJ_TILE = 512
ATT_TILE = 256
POST_TILE = 512
HEADS_PER_STEP = 2
VMEM_LIMIT = 56 * 1024 * 1024


def _lane_iota(shape):
    return lax.broadcasted_iota(jnp.int32, shape, len(shape) - 1)


def _rms(x, g):
    return x * lax.rsqrt(jnp.mean(x * x, axis=-1, keepdims=True) + EPS) * g


def _dot(a, b):
    return jnp.dot(a, b, preferred_element_type=F32)


def _dot_nt(a, b):
    return lax.dot_general(a, b, (((1,), (1,)), ((), ())), preferred_element_type=F32)


def _proj_kernel(x_ref, g_ref, win_ref, gq_ref, wqb_ref, gkv_ref, wkvb_ref,
                 cosa_ref, sina_ref, cosb_ref, sinb_ref,
                 qa_ref, ka_ref, va_ref, qm_ref, km_ref, vm_ref, sga_ref, sgb_ref):
    tm = x_ref.shape[1]
    h = _rms(x_ref[0], g_ref[...]).astype(BF16)
    lane = _lane_iota((tm, LANES))
    low_half = lane < HEAD_DIM
    ones_col = jnp.where(lane == AUX_LANE, 1.0, 0.0).astype(F32)
    row = lax.broadcasted_iota(jnp.int32, (tm, LANES), 0)
    blk = pl.program_id(1) * (tm // MOBA_BLOCK) + row // MOBA_BLOCK
    blk_onehot = jnp.where(lane == AUX_LANE + blk, 1.0, 0.0).astype(F32)

    cosa, sina = cosa_ref[...], sina_ref[...]
    first_a = (lane & (HEAD_DIM // 2)) == 0

    def rope_a(t):
        partner = jnp.where(first_a, pltpu.roll(t, LANES - HEAD_DIM // 2, 1),
                            pltpu.roll(t, HEAD_DIM // 2, 1))
        return t * cosa + partner * sina

    def split_pair(t):
        return (jnp.where(low_half, t, 0.0),
                jnp.where(low_half, pltpu.roll(t, HEAD_DIM, 1), 0.0))

    def moba_part(col0, out_ref, rope, scale, extra):
        t_all = _dot(h, win_ref[:, col0:col0 + MOBA_WIDTH])
        for g in range(HEADS // 2):
            t = t_all[:, g * LANES:(g + 1) * LANES]
            if rope:
                t = rope_a(t)
            if scale != 1.0:
                t = t * scale
            for k, th in enumerate(split_pair(t)):
                out_ref[0, 2 * g + k] = (th + extra).astype(BF16)

    moba_part(0, qa_ref, True, HEAD_DIM ** -0.5, 0.0)
    moba_part(MOBA_WIDTH, ka_ref, True, 1.0, blk_onehot)
    moba_part(2 * MOBA_WIDTH, va_ref, False, 1.0, ones_col)

    lat = _dot(h, win_ref[:, 3 * MOBA_WIDTH:3 * MOBA_WIDTH + 512])
    cosb, sinb = cosb_ref[...], sinb_ref[...]
    first_b = lane < AUX_LANE + MLA_ROPE_DIM // 2

    def rope_b(t):
        partner = jnp.where(first_b, pltpu.roll(t, LANES - MLA_ROPE_DIM // 2, 1),
                            pltpu.roll(t, MLA_ROPE_DIM // 2, 1))
        return t * cosb + partner * sinb

    cq = _rms(lat[:, :MLA_Q_RANK], gq_ref[...]).astype(BF16)
    qm = _dot(cq, wqb_ref[...])
    for hd in range(HEADS):
        t = rope_b(qm[:, hd * LANES:(hd + 1) * LANES]) * (MLA_QK_DIM ** -0.5)
        qm_ref[0, hd] = t.astype(BF16)

    ckv = _rms(lat[:, MLA_Q_RANK:MLA_Q_RANK + MLA_KV_RANK], gkv_ref[...]).astype(BF16)
    kr = rope_b(pltpu.roll(lat[:, MLA_Q_RANK + MLA_KV_RANK:], AUX_LANE, 1))
    kvm = _dot(ckv, wkvb_ref[...])
    for hd in range(HEADS):
        km_ref[0, hd] = (kvm[:, hd * LANES:(hd + 1) * LANES] + kr).astype(BF16)
        vm_ref[0, hd] = (kvm[:, (HEADS + hd) * LANES:(HEADS + hd + 1) * LANES]
                         + ones_col).astype(BF16)

    gate_col = 3 * MOBA_WIDTH + 512
    sga_ref[0] = jax.nn.sigmoid(_dot(h, win_ref[:, gate_col:gate_col + D_MODEL])).astype(BF16)
    sgb_ref[0] = jax.nn.sigmoid(
        _dot(h, win_ref[:, gate_col + D_MODEL:gate_col + 2 * D_MODEL])).astype(BF16)


def _softmax_step(q, k, v, m, acc, mask):
    s = _dot_nt(q, k)
    if mask is not None:
        s = jnp.where(mask, s, NEG_INF)
    m_new = jnp.maximum(m, jnp.max(s, axis=-1, keepdims=True))
    p = jnp.exp(s - m_new).astype(BF16)
    acc = jnp.exp(m - m_new) * acc + _dot(p, v)
    return m_new, acc


def _attend(q_list, k_ref, v_ref, o_ref):
    i = pl.program_id(2)
    tq = ATT_TILE
    r = lax.broadcasted_iota(jnp.int32, (tq, tq), 0)
    c = lax.broadcasted_iota(jnp.int32, (tq, tq), 1)
    causal = c <= r
    own = pl.multiple_of(i * tq, tq)
    state = []
    for hh, q in enumerate(q_list):
        m0 = jnp.full((tq, 1), NEG_INF, F32)
        a0 = jnp.zeros((tq, LANES), F32)
        state.extend(_softmax_step(q, k_ref[0, hh, pl.ds(own, tq), :],
                                   v_ref[0, hh, pl.ds(own, tq), :], m0, a0, causal))

    def body(j, st):
        off = pl.multiple_of(j * tq, tq)
        out = []
        for hh, q in enumerate(q_list):
            out.extend(_softmax_step(q, k_ref[0, hh, pl.ds(off, tq), :],
                                     v_ref[0, hh, pl.ds(off, tq), :],
                                     st[2 * hh], st[2 * hh + 1], None))
        return tuple(out)

    state = lax.fori_loop(0, i, body, tuple(state))
    outs = []
    for hh in range(len(q_list)):
        acc = state[2 * hh + 1]
        outs.append(acc[:, :HEAD_DIM] / acc[:, AUX_LANE:AUX_LANE + 1])
    o_ref[0] = jnp.concatenate(outs, axis=-1).astype(o_ref.dtype)


def _mla_kernel(q_ref, k_ref, v_ref, o_ref):
    _attend([q_ref[0, hh] for hh in range(HEADS_PER_STEP)], k_ref, v_ref, o_ref)


def _moba_kernel(q_ref, k_ref, v_ref, o_ref, kmean_ref):
    i = pl.program_id(2)
    n_blocks = k_ref.shape[2] // MOBA_BLOCK

    @pl.when(i == 0)
    def _():
        kmean_ref[...] = jnp.zeros(kmean_ref.shape, kmean_ref.dtype)
        for hh in range(HEADS_PER_STEP):
            kb = k_ref[0, hh].astype(F32).reshape(n_blocks, MOBA_BLOCK, LANES)
            kmean_ref[hh, AUX_LANE:AUX_LANE + n_blocks, :] = (
                jnp.sum(kb, axis=1) * (1.0 / MOBA_BLOCK)).astype(kmean_ref.dtype)

    lane = _lane_iota((ATT_TILE, LANES))
    lane_f = lane.astype(F32)
    past = (lane >= AUX_LANE) & (lane < AUX_LANE + i)
    q_list = []
    for hh in range(HEADS_PER_STEP):
        q = q_ref[0, hh]
        gate = jnp.where(past, _dot_nt(q, kmean_ref[hh]), NEG_INF)
        chosen = lane == AUX_LANE + i
        for _ in range(MOBA_TOPK):
            best = jnp.max(gate, axis=-1, keepdims=True)
            first = jnp.min(jnp.where(gate == best, lane_f, float(LANES)), axis=-1, keepdims=True)
            pick = (lane_f == first) & (best > NEG_INF * 0.5)
            chosen = chosen | pick
            gate = jnp.where(pick, NEG_INF, gate)
        bias = jnp.where(chosen, 0.0, NEG_INF)
        in_aux = (lane >= AUX_LANE) & (lane < AUX_LANE + n_blocks)
        q_list.append(jnp.where(in_aux, bias, q.astype(F32)).astype(BF16))
    _attend(q_list, k_ref, v_ref, o_ref)


def _post_kernel(x_ref, oa_ref, ob_ref, sga_ref, sgb_ref, p_ref,
                 wa_ref, wb_ref, wo_ref, gf_ref, wg_ref, wu_ref, wd_ref,
                 gp_ref, wpg_ref, wpp_ref, gn_ref, out_ref):
    x = x_ref[0]
    ya = _dot(oa_ref[0], wa_ref[...])
    yb = _dot(ob_ref[0], wb_ref[...])
    mix = sga_ref[0].astype(F32) * ya + sgb_ref[0].astype(F32) * yb
    x = x + _dot(mix.astype(BF16), wo_ref[...])

    h = _rms(x, gf_ref[...]).astype(BF16)

    def ffn_chunk(c, acc):
        g = _dot(h, wg_ref[c])
        u = _dot(h, wu_ref[c])
        a = (g * jax.nn.sigmoid(g) * u).astype(BF16)
        return acc + _dot(a, wd_ref[c])

    x = x + lax.fori_loop(0, N_FF_CHUNKS, ffn_chunk, jnp.zeros(x.shape, F32))

    h = _rms(x, gp_ref[...]).astype(BF16)
    x = x + jax.nn.sigmoid(_dot(h, wpg_ref[...])) * _dot(p_ref[0].astype(BF16), wpp_ref[...])
    out_ref[0] = _rms(x, gn_ref[...])


def _rope_tables(seq):
    pos = jnp.arange(seq, dtype=F32)[:, None]

    def cos_sin(d):
        half = d // 2
        inv_freq = 1.0 / (ROPE_THETA ** (jnp.arange(half, dtype=F32) * (2.0 / d)))
        ang = pos * inv_freq[None, :]
        return jnp.cos(ang), jnp.sin(ang)

    ca, sa = cos_sin(HEAD_DIM)
    cosa = jnp.concatenate([ca, ca, ca, ca], axis=1)
    sina = jnp.concatenate([-sa, sa, -sa, sa], axis=1)
    cb, sb = cos_sin(MLA_ROPE_DIM)
    one = jnp.ones((seq, HEAD_DIM), F32)
    cosb = jnp.concatenate([one, cb, cb, one[:, :LANES - MLA_QK_DIM]], axis=1)
    sinb = jnp.concatenate([0 * one, -sb, sb, 0 * one[:, :LANES - MLA_QK_DIM]], axis=1)
    return cosa, sina, cosb, sinb


def _pad_heads(w, per_head, lo, hi):
    k = w.shape[0]
    w = w.reshape(k, HEADS, per_head)[:, :, lo:hi]
    return jnp.pad(w, ((0, 0), (0, 0), (0, LANES - (hi - lo)))).reshape(k, HEADS * LANES)


def _const_spec(shape):
    zeros = (0,) * len(shape)
    return pl.BlockSpec(shape, lambda *_: zeros, pipeline_mode=pl.Buffered(1))


def _params(n_axes):
    return pltpu.CompilerParams(dimension_semantics=("arbitrary",) * n_axes,
                                vmem_limit_bytes=VMEM_LIMIT)


def _layer(x, p, attn_norm, w_in, mla_q_norm, w_q_b, mla_kv_norm, w_kv_b,
           w_moba_branch, w_mla_branch, w_o, ffn_norm, w_gate_up, w_down,
           ple_norm, w_ple_gate, w_ple_proj, out_norm, tables):
    B, S, _ = x.shape
    cosa, sina, cosb, sinb = tables

    kr_end = 3 * MOBA_WIDTH + MLA_Q_RANK + MLA_KV_RANK + MLA_ROPE_DIM
    w_in_p = jnp.concatenate(
        [w_in[:, :kr_end], jnp.zeros((D_MODEL, LANES - MLA_ROPE_DIM), w_in.dtype), w_in[:, kr_end:]],
        axis=1).astype(BF16)
    w_qb_p = _pad_heads(w_q_b, MLA_QK_DIM, 0, MLA_QK_DIM).astype(BF16)
    w_kvb_p = jnp.concatenate([_pad_heads(w_kv_b, 2 * HEAD_DIM, 0, HEAD_DIM),
                               _pad_heads(w_kv_b, 2 * HEAD_DIM, HEAD_DIM, 2 * HEAD_DIM)],
                              axis=1).astype(BF16)
    row = lambda g: g.reshape(1, -1).astype(F32)

    tm = PROJ_TILE
    head_shape = jax.ShapeDtypeStruct((B, HEADS, S, LANES), BF16)
    gate_shape = jax.ShapeDtypeStruct((B, S, D_MODEL), BF16)
    head_spec = pl.BlockSpec((1, HEADS, tm, LANES), lambda b, t: (b, 0, t, 0))
    tok_spec = lambda w: pl.BlockSpec((1, tm, w), lambda b, t: (b, t, 0))
    tab_spec = pl.BlockSpec((tm, LANES), lambda b, t: (t, 0))
    qa, ka, va, qm, km, vm, sga, sgb = pl.pallas_call(
        _proj_kernel,
        grid=(B, S // tm),
        in_specs=[tok_spec(D_MODEL), _const_spec((1, D_MODEL)), _const_spec(w_in_p.shape),
                  _const_spec((1, MLA_Q_RANK)), _const_spec(w_qb_p.shape),
                  _const_spec((1, MLA_KV_RANK)), _const_spec(w_kvb_p.shape),
                  tab_spec, tab_spec, tab_spec, tab_spec],
        out_specs=[head_spec] * 6 + [tok_spec(D_MODEL)] * 2,
        out_shape=[head_shape] * 6 + [gate_shape] * 2,
        compiler_params=_params(2),
        name="proj",
    )(x, row(attn_norm), w_in_p, row(mla_q_norm), w_qb_p, row(mla_kv_norm), w_kvb_p,
      cosa, sina, cosb, sinb)

    tq = ATT_TILE
    hp = HEADS_PER_STEP
    q_spec = pl.BlockSpec((1, hp, tq, LANES), lambda b, g, i: (b, g, i, 0))
    kv_spec = pl.BlockSpec((1, hp, S, LANES), lambda b, g, i: (b, g, 0, 0))
    o_spec = pl.BlockSpec((1, tq, hp * HEAD_DIM), lambda b, g, i: (b, i, g))
    att_shape = jax.ShapeDtypeStruct((B, S, MOBA_WIDTH), BF16)
    att_grid = (B, HEADS // hp, S // tq)
    oa = pl.pallas_call(
        _moba_kernel, grid=att_grid, in_specs=[q_spec, kv_spec, kv_spec], out_specs=o_spec,
        out_shape=att_shape, scratch_shapes=[pltpu.VMEM((hp, LANES, LANES), BF16)],
        compiler_params=_params(3), name="moba",
    )(qa, ka, va)
    ob = pl.pallas_call(
        _mla_kernel, grid=att_grid, in_specs=[q_spec, kv_spec, kv_spec], out_specs=o_spec,
        out_shape=att_shape, compiler_params=_params(3), name="mla",
    )(qm, km, vm)

    tp = POST_TILE
    w_g = w_gate_up[:, :D_FF].reshape(D_MODEL, N_FF_CHUNKS, FF_CHUNK).transpose(1, 0, 2).astype(BF16)
    w_u = w_gate_up[:, D_FF:].reshape(D_MODEL, N_FF_CHUNKS, FF_CHUNK).transpose(1, 0, 2).astype(BF16)
    w_d = w_down.reshape(N_FF_CHUNKS, FF_CHUNK, D_MODEL).astype(BF16)
    ptok = lambda w: pl.BlockSpec((1, tp, w), lambda b, t: (b, t, 0))
    weights = [w_moba_branch.astype(BF16), w_mla_branch.astype(BF16), w_o.astype(BF16),
               row(ffn_norm), w_g, w_u, w_d, row(ple_norm), w_ple_gate.astype(BF16),
               w_ple_proj.astype(BF16), row(out_norm)]
    return pl.pallas_call(
        _post_kernel,
        grid=(B, S // tp),
        in_specs=[ptok(D_MODEL), ptok(MOBA_WIDTH), ptok(MOBA_WIDTH), ptok(D_MODEL), ptok(D_MODEL),
                  ptok(PLE_DIM)] + [_const_spec(w.shape) for w in weights],
        out_specs=ptok(D_MODEL),
        out_shape=jax.ShapeDtypeStruct((B, S, D_MODEL), F32),
        compiler_params=_params(2),
        name="post",
    )(x, oa, ob, sga, sgb, p, *weights)


def kernel(x, p, attn_norm, w_in, mla_q_norm, w_q_b, mla_kv_norm, w_kv_b, w_moba_branch,
           w_mla_branch, w_o, ffn_norm, w_gate_up, w_down, ple_norm, w_ple_gate, w_ple_proj,
           final_norm):
    depth = p.shape[0]
    assert depth == 1, "the final norm is fused into the (single) layer's post kernel"
    tables = _rope_tables(x.shape[1])
    return _layer(x, p[0], attn_norm[0], w_in[0], mla_q_norm[0], w_q_b[0], mla_kv_norm[0],
                  w_kv_b[0], w_moba_branch[0], w_mla_branch[0], w_o[0], ffn_norm[0],
                  w_gate_up[0], w_down[0], ple_norm[0], w_ple_gate[0], w_ple_proj[0],
                  final_norm, tables)
```

```python
import jax
import jax.numpy as jnp
from jax import lax
from jax.experimental import pallas as pl
from jax.experimental.pallas import tpu as pltpu

F32 = jnp.float32
BF16 = jnp.bfloat16

D_MODEL = 1024
PLE_DIM = 256
ROPE_THETA = 10000.0
EPS = 1e-6
NEG_INF = -1e30

HEADS = 8
HEAD_DIM = 64
MOBA_WIDTH = HEADS * HEAD_DIM
MOBA_BLOCK = 256
MOBA_TOPK = 3
MLA_Q_RANK = 256
MLA_KV_RANK = 128
MLA_ROPE_DIM = 32
MLA_QK_DIM = HEAD_DIM + MLA_ROPE_DIM
D_FF = 2816

LANES = 128
AUX_LANE = HEAD_DIM
FF_CHUNK = 256
N_FF_CHUNKS = D_FF // FF_CHUNK
assert N_FF_CHUNKS * FF_CHUNK == D_FF

ATT_TILE = 512
PROJ_TILE = ATT_TILE
POST_TILE = 512
HEADS_PER_STEP = 2
VMEM_LIMIT = 56 * 1024 * 1024


def _lane_iota(shape):
    return lax.broadcasted_iota(jnp.int32, shape, len(shape) - 1)


def _rms(x, g):
    return x * lax.rsqrt(jnp.mean(x * x, axis=-1, keepdims=True) + EPS) * g


def _dot(a, b):
    return jnp.dot(a, b, preferred_element_type=F32)


def _dot_nt(a, b):
    return lax.dot_general(a, b, (((1,), (1,)), ((), ())), preferred_element_type=F32)


def _proj_kernel(x_ref, g_ref, win_ref, wvt_ref, gq_ref, wqb_ref, gkv_ref, wkb_ref, wvbt_ref,
                 cosa_ref, sina_ref, cosb_ref, sinb_ref,
                 qa_ref, ka_ref, vat_ref, qm_ref, km_ref, vmt_ref, sga_ref, sgb_ref):
    tm = x_ref.shape[1]
    h = _rms(x_ref[0], g_ref[...]).astype(BF16)
    lane = _lane_iota((tm, LANES))
    low_half = lane < HEAD_DIM
    row = lax.broadcasted_iota(jnp.int32, (tm, LANES), 0)
    blk = pl.program_id(1) * (tm // MOBA_BLOCK) + row // MOBA_BLOCK
    blk_onehot = jnp.where(lane == AUX_LANE + blk, 1.0, 0.0).astype(F32)
    ones_row = jnp.where(lax.broadcasted_iota(jnp.int32, (LANES, tm), 0) == AUX_LANE,
                         1.0, 0.0).astype(F32)

    cosa, sina = cosa_ref[...], sina_ref[...]
    first_a = (lane & (HEAD_DIM // 2)) == 0

    def rope_a(t):
        partner = jnp.where(first_a, pltpu.roll(t, LANES - HEAD_DIM // 2, 1),
                            pltpu.roll(t, HEAD_DIM // 2, 1))
        return t * cosa + partner * sina

    def moba_part(col0, out_ref, scale, extra):
        t_all = _dot(h, win_ref[:, col0:col0 + MOBA_WIDTH])
        for g in range(HEADS // 2):
            t = rope_a(t_all[:, g * LANES:(g + 1) * LANES])
            if scale != 1.0:
                t = t * scale
            out_ref[0, 2 * g] = (jnp.where(low_half, t, 0.0) + extra).astype(BF16)
            out_ref[0, 2 * g + 1] = (jnp.where(low_half, pltpu.roll(t, HEAD_DIM, 1), 0.0)
                                     + extra).astype(BF16)

    moba_part(0, qa_ref, HEAD_DIM ** -0.5, 0.0)
    moba_part(MOBA_WIDTH, ka_ref, 1.0, blk_onehot)
    vat = _dot_nt(wvt_ref[...], h)
    for hd in range(HEADS):
        vat_ref[0, hd, 0] = (vat[hd * LANES:(hd + 1) * LANES] + ones_row).astype(BF16)

    lat_col = 2 * MOBA_WIDTH
    lat = _dot(h, win_ref[:, lat_col:lat_col + 512])
    cosb, sinb = cosb_ref[...], sinb_ref[...]
    first_b = lane < AUX_LANE + MLA_ROPE_DIM // 2

    def rope_b(t):
        partner = jnp.where(first_b, pltpu.roll(t, LANES - MLA_ROPE_DIM // 2, 1),
                            pltpu.roll(t, MLA_ROPE_DIM // 2, 1))
        return t * cosb + partner * sinb

    cq = _rms(lat[:, :MLA_Q_RANK], gq_ref[...]).astype(BF16)
    qm = _dot(cq, wqb_ref[...])
    for hd in range(HEADS):
        t = rope_b(qm[:, hd * LANES:(hd + 1) * LANES]) * (MLA_QK_DIM ** -0.5)
        qm_ref[0, hd] = t.astype(BF16)

    ckv = _rms(lat[:, MLA_Q_RANK:MLA_Q_RANK + MLA_KV_RANK], gkv_ref[...]).astype(BF16)
    kr = rope_b(pltpu.roll(lat[:, MLA_Q_RANK + MLA_KV_RANK:], AUX_LANE, 1))
    kn = _dot(ckv, wkb_ref[...])
    vmt = _dot_nt(wvbt_ref[...], ckv)
    for hd in range(HEADS):
        km_ref[0, hd] = (kn[:, hd * LANES:(hd + 1) * LANES] + kr).astype(BF16)
        vmt_ref[0, hd, 0] = (vmt[hd * LANES:(hd + 1) * LANES] + ones_row).astype(BF16)

    gate_col = lat_col + 512
    sga_ref[0] = jax.nn.sigmoid(_dot(h, win_ref[:, gate_col:gate_col + D_MODEL])).astype(BF16)
    sgb_ref[0] = jax.nn.sigmoid(
        _dot(h, win_ref[:, gate_col + D_MODEL:gate_col + 2 * D_MODEL])).astype(BF16)


def _attend(q_refs, k_ref, vt_ref, o_ref, m_ref, acc_ref, sa_ref, sb_ref):
    i = pl.program_id(2)
    t = ATT_TILE
    n_heads = len(q_refs)
    for hh in range(n_heads):
        m_ref[hh] = jnp.full((1, t), NEG_INF, F32)
        acc_ref[hh] = jnp.zeros((LANES, t), F32)

    def scores(j, s_ref):
        off = pl.multiple_of(j * t, t)
        for hh in range(n_heads):
            s_ref[hh] = _dot_nt(k_ref[0, hh, pl.ds(off, t), :], q_refs[hh][...])

    def softmax_pv(j, s_ref, causal):
        for hh in range(n_heads):
            s = s_ref[hh]
            if causal:
                key = lax.broadcasted_iota(jnp.int32, (t, t), 0)
                qry = lax.broadcasted_iota(jnp.int32, (t, t), 1)
                s = jnp.where(key <= qry, s, NEG_INF)
            m_old = m_ref[hh]
            m_new = jnp.maximum(m_old, jnp.max(s, axis=0, keepdims=True))
            p = jnp.exp(s - m_new).astype(BF16)
            acc_ref[hh] = jnp.exp(m_old - m_new) * acc_ref[hh] + _dot(vt_ref[0, hh, j], p)
            m_ref[hh] = m_new

    scores(0, sa_ref)

    def body(jj, carry):
        j = 2 * jj
        scores(j + 1, sb_ref)
        softmax_pv(j, sa_ref, False)
        scores(j + 2, sa_ref)
        softmax_pv(j + 1, sb_ref, False)
        return carry

    lax.fori_loop(0, i // 2, body, 0)

    @pl.when(i % 2 == 0)
    def _():
        softmax_pv(i, sa_ref, True)

    @pl.when(i % 2 == 1)
    def _():
        scores(i, sb_ref)
        softmax_pv(i - 1, sa_ref, False)
        softmax_pv(i, sb_ref, True)

    outs = []
    for hh in range(n_heads):
        acc = acc_ref[hh]
        outs.append(acc[:HEAD_DIM] / acc[AUX_LANE:AUX_LANE + 1])
    o_ref[0] = jnp.concatenate(outs, axis=0).T.astype(o_ref.dtype)


def _mla_kernel(q_ref, k_ref, vt_ref, o_ref, m_ref, acc_ref, sa_ref, sb_ref):
    _attend([q_ref.at[0, hh] for hh in range(HEADS_PER_STEP)], k_ref, vt_ref, o_ref,
            m_ref, acc_ref, sa_ref, sb_ref)


def _moba_kernel(q_ref, k_ref, vt_ref, o_ref, m_ref, acc_ref, sa_ref, sb_ref, kmean_ref,
                 qaug_ref):
    i = pl.program_id(2)
    n_blocks = k_ref.shape[2] // MOBA_BLOCK

    @pl.when(i == 0)
    def _():
        kmean_ref[...] = jnp.zeros(kmean_ref.shape, kmean_ref.dtype)
        for hh in range(HEADS_PER_STEP):
            kb = k_ref[0, hh].astype(F32).reshape(n_blocks, MOBA_BLOCK, LANES)
            kmean_ref[hh, AUX_LANE:AUX_LANE + n_blocks, :] = (
                jnp.sum(kb, axis=1) * (1.0 / MOBA_BLOCK)).astype(kmean_ref.dtype)

    lane = _lane_iota((ATT_TILE, LANES))
    lane_f = lane.astype(F32)
    row = lax.broadcasted_iota(jnp.int32, (ATT_TILE, LANES), 0)
    cur = AUX_LANE + i * (ATT_TILE // MOBA_BLOCK) + row // MOBA_BLOCK
    past = (lane >= AUX_LANE) & (lane < cur)
    in_aux = (lane >= AUX_LANE) & (lane < AUX_LANE + n_blocks)
    for hh in range(HEADS_PER_STEP):
        q = q_ref[0, hh]
        gate = jnp.where(past, _dot_nt(q, kmean_ref[hh]), NEG_INF)
        chosen = lane == cur
        for _ in range(MOBA_TOPK):
            best = jnp.max(gate, axis=-1, keepdims=True)
            first = jnp.min(jnp.where(gate == best, lane_f, float(LANES)), axis=-1, keepdims=True)
            pick = (lane_f == first) & (best > NEG_INF * 0.5)
            chosen = chosen | pick
            gate = jnp.where(pick, NEG_INF, gate)
        bias = jnp.where(chosen, 0.0, NEG_INF)
        qaug_ref[hh] = jnp.where(in_aux, bias, q.astype(F32)).astype(BF16)
    _attend([qaug_ref.at[hh] for hh in range(HEADS_PER_STEP)], k_ref, vt_ref, o_ref,
            m_ref, acc_ref, sa_ref, sb_ref)


def _post_kernel(x_ref, oa_ref, ob_ref, sga_ref, sgb_ref, p_ref,
                 wa_ref, wb_ref, wo_ref, gf_ref, wg_ref, wu_ref, wd_ref,
                 gp_ref, wpg_ref, wpp_ref, gn_ref, out_ref):
    x = x_ref[0]
    ya = _dot(oa_ref[0], wa_ref[...])
    yb = _dot(ob_ref[0], wb_ref[...])
    mix = sga_ref[0].astype(F32) * ya + sgb_ref[0].astype(F32) * yb
    x = x + _dot(mix.astype(BF16), wo_ref[...])

    h = _rms(x, gf_ref[...]).astype(BF16)

    def ffn_chunk(c, acc):
        g = _dot(h, wg_ref[c])
        u = _dot(h, wu_ref[c])
        a = (g * jax.nn.sigmoid(g) * u).astype(BF16)
        return acc + _dot(a, wd_ref[c])

    x = x + lax.fori_loop(0, N_FF_CHUNKS, ffn_chunk, jnp.zeros(x.shape, F32))

    h = _rms(x, gp_ref[...]).astype(BF16)
    x = x + jax.nn.sigmoid(_dot(h, wpg_ref[...])) * _dot(p_ref[0].astype(BF16), wpp_ref[...])
    out_ref[0] = _rms(x, gn_ref[...])


def _rope_tables(seq):
    pos = jnp.arange(seq, dtype=F32)[:, None]

    def cos_sin(d):
        half = d // 2
        inv_freq = 1.0 / (ROPE_THETA ** (jnp.arange(half, dtype=F32) * (2.0 / d)))
        ang = pos * inv_freq[None, :]
        return jnp.cos(ang), jnp.sin(ang)

    ca, sa = cos_sin(HEAD_DIM)
    cosa = jnp.concatenate([ca, ca, ca, ca], axis=1)
    sina = jnp.concatenate([-sa, sa, -sa, sa], axis=1)
    cb, sb = cos_sin(MLA_ROPE_DIM)
    one = jnp.ones((seq, HEAD_DIM), F32)
    cosb = jnp.concatenate([one, cb, cb, one[:, :LANES - MLA_QK_DIM]], axis=1)
    sinb = jnp.concatenate([0 * one, -sb, sb, 0 * one[:, :LANES - MLA_QK_DIM]], axis=1)
    return cosa, sina, cosb, sinb


def _pad_heads(w, per_head, lo, hi):
    k = w.shape[0]
    w = w.reshape(k, HEADS, per_head)[:, :, lo:hi]
    return jnp.pad(w, ((0, 0), (0, 0), (0, LANES - (hi - lo)))).reshape(k, HEADS * LANES)


def _const_spec(shape):
    zeros = (0,) * len(shape)
    return pl.BlockSpec(shape, lambda *_: zeros, pipeline_mode=pl.Buffered(1))


def _params(n_axes):
    return pltpu.CompilerParams(dimension_semantics=("arbitrary",) * n_axes,
                                vmem_limit_bytes=VMEM_LIMIT)


def _layer(x, p, attn_norm, w_in, mla_q_norm, w_q_b, mla_kv_norm, w_kv_b,
           w_moba_branch, w_mla_branch, w_o, ffn_norm, w_gate_up, w_down,
           ple_norm, w_ple_gate, w_ple_proj, out_norm, tables):
    B, S, _ = x.shape
    cosa, sina, cosb, sinb = tables

    v_lo, v_hi = 2 * MOBA_WIDTH, 3 * MOBA_WIDTH
    kr_end = v_hi + MLA_Q_RANK + MLA_KV_RANK + MLA_ROPE_DIM
    w_in_p = jnp.concatenate(
        [w_in[:, :v_lo], w_in[:, v_hi:kr_end],
         jnp.zeros((D_MODEL, LANES - MLA_ROPE_DIM), w_in.dtype), w_in[:, kr_end:]],
        axis=1).astype(BF16)
    w_vt_p = _pad_heads(w_in[:, v_lo:v_hi], HEAD_DIM, 0, HEAD_DIM).T.astype(BF16)
    w_qb_p = _pad_heads(w_q_b, MLA_QK_DIM, 0, MLA_QK_DIM).astype(BF16)
    w_kb_p = _pad_heads(w_kv_b, 2 * HEAD_DIM, 0, HEAD_DIM).astype(BF16)
    w_vbt_p = _pad_heads(w_kv_b, 2 * HEAD_DIM, HEAD_DIM, 2 * HEAD_DIM).T.astype(BF16)
    row = lambda g: g.reshape(1, -1).astype(F32)

    tm = PROJ_TILE
    head_shape = jax.ShapeDtypeStruct((B, HEADS, S, LANES), BF16)
    vt_shape = jax.ShapeDtypeStruct((B, HEADS, S // tm, LANES, tm), BF16)
    gate_shape = jax.ShapeDtypeStruct((B, S, D_MODEL), BF16)
    head_spec = pl.BlockSpec((1, HEADS, tm, LANES), lambda b, t: (b, 0, t, 0))
    vt_spec = pl.BlockSpec((1, HEADS, 1, LANES, tm), lambda b, t: (b, 0, t, 0, 0))
    tok_spec = lambda w: pl.BlockSpec((1, tm, w), lambda b, t: (b, t, 0))
    tab_spec = pl.BlockSpec((tm, LANES), lambda b, t: (t, 0))
    qa, ka, vat, qm, km, vmt, sga, sgb = pl.pallas_call(
        _proj_kernel,
        grid=(B, S // tm),
        in_specs=[tok_spec(D_MODEL), _const_spec((1, D_MODEL)), _const_spec(w_in_p.shape),
                  _const_spec(w_vt_p.shape),
                  _const_spec((1, MLA_Q_RANK)), _const_spec(w_qb_p.shape),
                  _const_spec((1, MLA_KV_RANK)), _const_spec(w_kb_p.shape),
                  _const_spec(w_vbt_p.shape),
                  tab_spec, tab_spec, tab_spec, tab_spec],
        out_specs=[head_spec, head_spec, vt_spec, head_spec, head_spec, vt_spec,
                   tok_spec(D_MODEL), tok_spec(D_MODEL)],
        out_shape=[head_shape, head_shape, vt_shape, head_shape, head_shape, vt_shape,
                   gate_shape, gate_shape],
        compiler_params=_params(2),
        name="proj",
    )(x, row(attn_norm), w_in_p, w_vt_p, row(mla_q_norm), w_qb_p, row(mla_kv_norm), w_kb_p,
      w_vbt_p, cosa, sina, cosb, sinb)

    t = ATT_TILE
    hp = HEADS_PER_STEP
    q_spec = pl.BlockSpec((1, hp, t, LANES), lambda b, g, i: (b, g, i, 0))
    k_spec = pl.BlockSpec((1, hp, S, LANES), lambda b, g, i: (b, g, 0, 0))
    vt_all = pl.BlockSpec((1, hp, S // t, LANES, t), lambda b, g, i: (b, g, 0, 0, 0))
    o_spec = pl.BlockSpec((1, t, hp * HEAD_DIM), lambda b, g, i: (b, i, g))
    att_shape = jax.ShapeDtypeStruct((B, S, MOBA_WIDTH), BF16)
    att_grid = (B, HEADS // hp, S // t)
    stats = [pltpu.VMEM((hp, 1, t), F32), pltpu.VMEM((hp, LANES, t), F32),
             pltpu.VMEM((hp, t, t), F32), pltpu.VMEM((hp, t, t), F32)]
    oa = pl.pallas_call(
        _moba_kernel, grid=att_grid, in_specs=[q_spec, k_spec, vt_all], out_specs=o_spec,
        out_shape=att_shape,
        scratch_shapes=stats + [pltpu.VMEM((hp, LANES, LANES), BF16),
                                pltpu.VMEM((hp, t, LANES), BF16)],
        compiler_params=_params(3), name="moba",
    )(qa, ka, vat)
    ob = pl.pallas_call(
        _mla_kernel, grid=att_grid, in_specs=[q_spec, k_spec, vt_all], out_specs=o_spec,
        out_shape=att_shape, scratch_shapes=stats, compiler_params=_params(3), name="mla",
    )(qm, km, vmt)

    tp = POST_TILE
    w_g = w_gate_up[:, :D_FF].reshape(D_MODEL, N_FF_CHUNKS, FF_CHUNK).transpose(1, 0, 2).astype(BF16)
    w_u = w_gate_up[:, D_FF:].reshape(D_MODEL, N_FF_CHUNKS, FF_CHUNK).transpose(1, 0, 2).astype(BF16)
    w_d = w_down.reshape(N_FF_CHUNKS, FF_CHUNK, D_MODEL).astype(BF16)
    ptok = lambda w: pl.BlockSpec((1, tp, w), lambda b, t: (b, t, 0))
    weights = [w_moba_branch.astype(BF16), w_mla_branch.astype(BF16), w_o.astype(BF16),
               row(ffn_norm), w_g, w_u, w_d, row(ple_norm), w_ple_gate.astype(BF16),
               w_ple_proj.astype(BF16), row(out_norm)]
    return pl.pallas_call(
        _post_kernel,
        grid=(B, S // tp),
        in_specs=[ptok(D_MODEL), ptok(MOBA_WIDTH), ptok(MOBA_WIDTH), ptok(D_MODEL), ptok(D_MODEL),
                  ptok(PLE_DIM)] + [_const_spec(w.shape) for w in weights],
        out_specs=ptok(D_MODEL),
        out_shape=jax.ShapeDtypeStruct((B, S, D_MODEL), F32),
        compiler_params=_params(2),
        name="post",
    )(x, oa, ob, sga, sgb, p, *weights)


def kernel(x, p, attn_norm, w_in, mla_q_norm, w_q_b, mla_kv_norm, w_kv_b, w_moba_branch,
           w_mla_branch, w_o, ffn_norm, w_gate_up, w_down, ple_norm, w_ple_gate, w_ple_proj,
           final_norm):
    depth = p.shape[0]
    assert depth == 1, "the final norm is fused into the (single) layer's post kernel"
    tables = _rope_tables(x.shape[1])
    return _layer(x, p[0], attn_norm[0], w_in[0], mla_q_norm[0], w_q_b[0], mla_kv_norm[0],
                  w_kv_b[0], w_moba_branch[0], w_mla_branch[0], w_o[0], ffn_norm[0],
                  w_gate_up[0], w_down[0], ple_norm[0], w_ple_gate[0], w_ple_proj[0],
                  final_norm, tables)
```

```python
import jax
import jax.numpy as jnp
from jax import lax
from jax.experimental import pallas as pl
from jax.experimental.pallas import tpu as pltpu

F32 = jnp.float32
BF16 = jnp.bfloat16

D_MODEL = 1024
PLE_DIM = 256
ROPE_THETA = 10000.0
EPS = 1e-6
NEG_INF = -1e30
LOG2_E = 1.4426950408889634

HEADS = 8
HEAD_DIM = 64
MOBA_WIDTH = HEADS * HEAD_DIM
MOBA_BLOCK = 256
MOBA_TOPK = 3
MLA_Q_RANK = 256
MLA_KV_RANK = 128
MLA_ROPE_DIM = 32
MLA_QK_DIM = HEAD_DIM + MLA_ROPE_DIM
D_FF = 2816

LANES = 128
AUX_LANE = HEAD_DIM
FF_CHUNK = 256
N_FF_CHUNKS = D_FF // FF_CHUNK
assert N_FF_CHUNKS * FF_CHUNK == D_FF

ATT_TILE = 512
PROJ_TILE = ATT_TILE
POST_TILE = 512
HEADS_PER_STEP = 4
VMEM_LIMIT = 56 * 1024 * 1024


def _lane_iota(shape):
    return lax.broadcasted_iota(jnp.int32, shape, len(shape) - 1)


def _rms(x, g):
    return x * lax.rsqrt(jnp.mean(x * x, axis=-1, keepdims=True) + EPS) * g


def _dot(a, b):
    return jnp.dot(a, b, preferred_element_type=F32)


def _dot_nt(a, b):
    return lax.dot_general(a, b, (((1,), (1,)), ((), ())), preferred_element_type=F32)


def _proj_kernel(x_ref, g_ref, win_ref, wvt_ref, gq_ref, wqb_ref, gkv_ref, wkb_ref, wvbt_ref,
                 cosa_ref, sina_ref, cosb_ref, sinb_ref,
                 qa_ref, ka_ref, vat_ref, qm_ref, km_ref, vmt_ref, sga_ref, sgb_ref):
    tm = x_ref.shape[1]
    h = _rms(x_ref[0], g_ref[...]).astype(BF16)
    lane = _lane_iota((tm, LANES))
    low_half = lane < HEAD_DIM
    row = lax.broadcasted_iota(jnp.int32, (tm, LANES), 0)
    blk = pl.program_id(1) * (tm // MOBA_BLOCK) + row // MOBA_BLOCK
    blk_onehot = jnp.where(lane == AUX_LANE + blk, 1.0, 0.0).astype(F32)
    ones_row = jnp.where(lax.broadcasted_iota(jnp.int32, (LANES, tm), 0) == AUX_LANE,
                         1.0, 0.0).astype(F32)

    cosa, sina = cosa_ref[...], sina_ref[...]
    first_a = (lane & (HEAD_DIM // 2)) == 0

    def rope_a(t):
        partner = jnp.where(first_a, pltpu.roll(t, LANES - HEAD_DIM // 2, 1),
                            pltpu.roll(t, HEAD_DIM // 2, 1))
        return t * cosa + partner * sina

    def moba_part(col0, out_ref, scale, extra):
        t_all = _dot(h, win_ref[:, col0:col0 + MOBA_WIDTH])
        for g in range(HEADS // 2):
            t = rope_a(t_all[:, g * LANES:(g + 1) * LANES])
            if scale != 1.0:
                t = t * scale
            out_ref[0, 2 * g] = (jnp.where(low_half, t, 0.0) + extra).astype(BF16)
            out_ref[0, 2 * g + 1] = (jnp.where(low_half, pltpu.roll(t, HEAD_DIM, 1), 0.0)
                                     + extra).astype(BF16)

    moba_part(0, qa_ref, HEAD_DIM ** -0.5 * LOG2_E, 0.0)
    moba_part(MOBA_WIDTH, ka_ref, 1.0, blk_onehot)
    vat = _dot_nt(wvt_ref[...], h)
    for hd in range(HEADS):
        vat_ref[0, hd, 0] = (vat[hd * LANES:(hd + 1) * LANES] + ones_row).astype(BF16)

    lat_col = 2 * MOBA_WIDTH
    lat = _dot(h, win_ref[:, lat_col:lat_col + 512])
    cosb, sinb = cosb_ref[...], sinb_ref[...]
    first_b = lane < AUX_LANE + MLA_ROPE_DIM // 2

    def rope_b(t):
        partner = jnp.where(first_b, pltpu.roll(t, LANES - MLA_ROPE_DIM // 2, 1),
                            pltpu.roll(t, MLA_ROPE_DIM // 2, 1))
        return t * cosb + partner * sinb

    cq = _rms(lat[:, :MLA_Q_RANK], gq_ref[...]).astype(BF16)
    qm = _dot(cq, wqb_ref[...])
    for hd in range(HEADS):
        t = rope_b(qm[:, hd * LANES:(hd + 1) * LANES]) * (MLA_QK_DIM ** -0.5 * LOG2_E)
        qm_ref[0, hd] = t.astype(BF16)

    ckv = _rms(lat[:, MLA_Q_RANK:MLA_Q_RANK + MLA_KV_RANK], gkv_ref[...]).astype(BF16)
    kr = rope_b(pltpu.roll(lat[:, MLA_Q_RANK + MLA_KV_RANK:], AUX_LANE, 1))
    kn = _dot(ckv, wkb_ref[...])
    vmt = _dot_nt(wvbt_ref[...], ckv)
    for hd in range(HEADS):
        km_ref[0, hd] = (kn[:, hd * LANES:(hd + 1) * LANES] + kr).astype(BF16)
        vmt_ref[0, hd, 0] = (vmt[hd * LANES:(hd + 1) * LANES] + ones_row).astype(BF16)

    gate_col = lat_col + 512
    sga_ref[0] = jax.nn.sigmoid(_dot(h, win_ref[:, gate_col:gate_col + D_MODEL])).astype(BF16)
    sgb_ref[0] = jax.nn.sigmoid(
        _dot(h, win_ref[:, gate_col + D_MODEL:gate_col + 2 * D_MODEL])).astype(BF16)


def _attend(q_refs, k_ref, vt_ref, o_ref, m_ref, acc_ref, buf_a, buf_b):
    i = pl.program_id(2)
    t = ATT_TILE
    n_heads = len(q_refs)
    for hh in range(n_heads):
        m_ref[hh] = jnp.full((1, t), NEG_INF, F32)
        acc_ref[hh] = jnp.zeros((LANES, t), F32)

    def scores(j, buf, causal=False):
        s_ref, mx_ref = buf
        off = pl.multiple_of(j * t, t)
        for hh in range(n_heads):
            s = _dot_nt(k_ref[0, hh, pl.ds(off, t), :], q_refs[hh][...])
            if causal:
                key = lax.broadcasted_iota(jnp.int32, (t, t), 0)
                qry = lax.broadcasted_iota(jnp.int32, (t, t), 1)
                s = jnp.where(key <= qry, s, NEG_INF)
            s_ref[hh] = s
            mx_ref[hh] = jnp.max(s, axis=0, keepdims=True)

    def softmax_pv(j, buf):
        s_ref, mx_ref = buf
        for hh in range(n_heads):
            m_old = m_ref[hh]
            m_new = jnp.maximum(m_old, mx_ref[hh])
            p = jnp.exp2(s_ref[hh] - m_new).astype(BF16)
            acc_ref[hh] = jnp.exp2(m_old - m_new) * acc_ref[hh] + _dot(vt_ref[0, hh, j], p)
            m_ref[hh] = m_new

    @pl.when(i == 0)
    def _():
        scores(0, buf_a, causal=True)
        softmax_pv(0, buf_a)

    @pl.when(i > 0)
    def _():
        scores(0, buf_a)

        def body(jj, carry):
            j = 2 * jj
            scores(j + 1, buf_b)
            softmax_pv(j, buf_a)
            scores(j + 2, buf_a)
            softmax_pv(j + 1, buf_b)
            return carry

        lax.fori_loop(0, (i - 1) // 2, body, 0)

        @pl.when(i % 2 == 1)
        def _():
            scores(i, buf_b, causal=True)
            softmax_pv(i - 1, buf_a)
            softmax_pv(i, buf_b)

        @pl.when(i % 2 == 0)
        def _():
            scores(i - 1, buf_b)
            softmax_pv(i - 2, buf_a)
            scores(i, buf_a, causal=True)
            softmax_pv(i - 1, buf_b)
            softmax_pv(i, buf_a)

    outs = []
    for hh in range(n_heads):
        acc = acc_ref[hh]
        outs.append(acc[:HEAD_DIM] / acc[AUX_LANE:AUX_LANE + 1])
    o_ref[0] = jnp.concatenate(outs, axis=0).T.astype(o_ref.dtype)


def _mla_kernel(q_ref, k_ref, vt_ref, o_ref, m_ref, acc_ref, sa_ref, sb_ref, mxa_ref, mxb_ref):
    _attend([q_ref.at[0, hh] for hh in range(HEADS_PER_STEP)], k_ref, vt_ref, o_ref,
            m_ref, acc_ref, (sa_ref, mxa_ref), (sb_ref, mxb_ref))


def _moba_kernel(q_ref, k_ref, vt_ref, o_ref, m_ref, acc_ref, sa_ref, sb_ref, mxa_ref, mxb_ref,
                 kmean_ref, qaug_ref):
    i = pl.program_id(2)
    t = ATT_TILE
    n_blocks = k_ref.shape[2] // MOBA_BLOCK

    @pl.when(i == 0)
    def _():
        kmean_ref[...] = jnp.zeros(kmean_ref.shape, kmean_ref.dtype)
        for hh in range(HEADS_PER_STEP):
            kb = k_ref[0, hh].astype(F32).reshape(n_blocks, MOBA_BLOCK, LANES)
            kmean_ref[hh, AUX_LANE:AUX_LANE + n_blocks, :] = (
                jnp.sum(kb, axis=1) * (1.0 / MOBA_BLOCK)).astype(kmean_ref.dtype)

    blk = lax.broadcasted_iota(jnp.int32, (n_blocks, t), 0)
    blk_f = blk.astype(F32)
    qcol = lax.broadcasted_iota(jnp.int32, (n_blocks, t), 1)
    cur = i * (t // MOBA_BLOCK) + qcol // MOBA_BLOCK
    past = blk < cur
    for hh in range(HEADS_PER_STEP):
        q = q_ref[0, hh]
        gate = _dot_nt(kmean_ref[hh], q)[AUX_LANE:AUX_LANE + n_blocks]
        gate = jnp.where(past, gate, NEG_INF)
        chosen = blk == cur
        for _ in range(MOBA_TOPK):
            best = jnp.max(gate, axis=0, keepdims=True)
            first = jnp.min(jnp.where(gate == best, blk_f, float(n_blocks)), axis=0,
                            keepdims=True)
            pick = (blk_f == first) & (best > NEG_INF * 0.5)
            chosen = chosen | pick
            gate = jnp.where(pick, NEG_INF, gate)
        bias = jnp.concatenate(
            [jnp.zeros((AUX_LANE, t), F32), jnp.where(chosen, 0.0, NEG_INF),
             jnp.zeros((LANES - AUX_LANE - n_blocks, t), F32)], axis=0).T
        qaug_ref[hh] = (q.astype(F32) + bias).astype(BF16)
    _attend([qaug_ref.at[hh] for hh in range(HEADS_PER_STEP)], k_ref, vt_ref, o_ref,
            m_ref, acc_ref, (sa_ref, mxa_ref), (sb_ref, mxb_ref))


def _post_kernel(x_ref, oa_ref, ob_ref, sga_ref, sgb_ref, p_ref,
                 wa_ref, wb_ref, wo_ref, gf_ref, wg_ref, wu_ref, wd_ref,
                 gp_ref, wpg_ref, wpp_ref, gn_ref, out_ref):
    x = x_ref[0]
    ya = _dot(oa_ref[0], wa_ref[...])
    yb = _dot(ob_ref[0], wb_ref[...])
    mix = sga_ref[0].astype(F32) * ya + sgb_ref[0].astype(F32) * yb
    x = x + _dot(mix.astype(BF16), wo_ref[...])

    h = _rms(x, gf_ref[...]).astype(BF16)

    def ffn_chunk(c, acc):
        g = _dot(h, wg_ref[c])
        u = _dot(h, wu_ref[c])
        a = (g * jax.nn.sigmoid(g) * u).astype(BF16)
        return acc + _dot(a, wd_ref[c])

    x = x + lax.fori_loop(0, N_FF_CHUNKS, ffn_chunk, jnp.zeros(x.shape, F32))

    h = _rms(x, gp_ref[...]).astype(BF16)
    x = x + jax.nn.sigmoid(_dot(h, wpg_ref[...])) * _dot(p_ref[0].astype(BF16), wpp_ref[...])
    out_ref[0] = _rms(x, gn_ref[...])


def _rope_tables(seq):
    pos = jnp.arange(seq, dtype=F32)[:, None]

    def cos_sin(d):
        half = d // 2
        inv_freq = 1.0 / (ROPE_THETA ** (jnp.arange(half, dtype=F32) * (2.0 / d)))
        ang = pos * inv_freq[None, :]
        return jnp.cos(ang), jnp.sin(ang)

    ca, sa = cos_sin(HEAD_DIM)
    cosa = jnp.concatenate([ca, ca, ca, ca], axis=1)
    sina = jnp.concatenate([-sa, sa, -sa, sa], axis=1)
    cb, sb = cos_sin(MLA_ROPE_DIM)
    one = jnp.ones((seq, HEAD_DIM), F32)
    cosb = jnp.concatenate([one, cb, cb, one[:, :LANES - MLA_QK_DIM]], axis=1)
    sinb = jnp.concatenate([0 * one, -sb, sb, 0 * one[:, :LANES - MLA_QK_DIM]], axis=1)
    return cosa, sina, cosb, sinb


def _pad_heads(w, per_head, lo, hi):
    k = w.shape[0]
    w = w.reshape(k, HEADS, per_head)[:, :, lo:hi]
    return jnp.pad(w, ((0, 0), (0, 0), (0, LANES - (hi - lo)))).reshape(k, HEADS * LANES)


def _const_spec(shape):
    zeros = (0,) * len(shape)
    return pl.BlockSpec(shape, lambda *_: zeros, pipeline_mode=pl.Buffered(1))


def _params(n_axes):
    return pltpu.CompilerParams(dimension_semantics=("arbitrary",) * n_axes,
                                vmem_limit_bytes=VMEM_LIMIT)


def _layer(x, p, attn_norm, w_in, mla_q_norm, w_q_b, mla_kv_norm, w_kv_b,
           w_moba_branch, w_mla_branch, w_o, ffn_norm, w_gate_up, w_down,
           ple_norm, w_ple_gate, w_ple_proj, out_norm, tables):
    B, S, _ = x.shape
    cosa, sina, cosb, sinb = tables

    v_lo, v_hi = 2 * MOBA_WIDTH, 3 * MOBA_WIDTH
    kr_end = v_hi + MLA_Q_RANK + MLA_KV_RANK + MLA_ROPE_DIM
    w_in_p = jnp.concatenate(
        [w_in[:, :v_lo], w_in[:, v_hi:kr_end],
         jnp.zeros((D_MODEL, LANES - MLA_ROPE_DIM), w_in.dtype), w_in[:, kr_end:]],
        axis=1).astype(BF16)
    w_vt_p = _pad_heads(w_in[:, v_lo:v_hi], HEAD_DIM, 0, HEAD_DIM).T.astype(BF16)
    w_qb_p = _pad_heads(w_q_b, MLA_QK_DIM, 0, MLA_QK_DIM).astype(BF16)
    w_kb_p = _pad_heads(w_kv_b, 2 * HEAD_DIM, 0, HEAD_DIM).astype(BF16)
    w_vbt_p = _pad_heads(w_kv_b, 2 * HEAD_DIM, HEAD_DIM, 2 * HEAD_DIM).T.astype(BF16)
    row = lambda g: g.reshape(1, -1).astype(F32)

    tm = PROJ_TILE
    head_shape = jax.ShapeDtypeStruct((B, HEADS, S, LANES), BF16)
    vt_shape = jax.ShapeDtypeStruct((B, HEADS, S // tm, LANES, tm), BF16)
    gate_shape = jax.ShapeDtypeStruct((B, S, D_MODEL), BF16)
    head_spec = pl.BlockSpec((1, HEADS, tm, LANES), lambda b, t: (b, 0, t, 0))
    vt_spec = pl.BlockSpec((1, HEADS, 1, LANES, tm), lambda b, t: (b, 0, t, 0, 0))
    tok_spec = lambda w: pl.BlockSpec((1, tm, w), lambda b, t: (b, t, 0))
    tab_spec = pl.BlockSpec((tm, LANES), lambda b, t: (t, 0))
    qa, ka, vat, qm, km, vmt, sga, sgb = pl.pallas_call(
        _proj_kernel,
        grid=(B, S // tm),
        in_specs=[tok_spec(D_MODEL), _const_spec((1, D_MODEL)), _const_spec(w_in_p.shape),
                  _const_spec(w_vt_p.shape),
                  _const_spec((1, MLA_Q_RANK)), _const_spec(w_qb_p.shape),
                  _const_spec((1, MLA_KV_RANK)), _const_spec(w_kb_p.shape),
                  _const_spec(w_vbt_p.shape),
                  tab_spec, tab_spec, tab_spec, tab_spec],
        out_specs=[head_spec, head_spec, vt_spec, head_spec, head_spec, vt_spec,
                   tok_spec(D_MODEL), tok_spec(D_MODEL)],
        out_shape=[head_shape, head_shape, vt_shape, head_shape, head_shape, vt_shape,
                   gate_shape, gate_shape],
        compiler_params=_params(2),
        name="proj",
    )(x, row(attn_norm), w_in_p, w_vt_p, row(mla_q_norm), w_qb_p, row(mla_kv_norm), w_kb_p,
      w_vbt_p, cosa, sina, cosb, sinb)

    t = ATT_TILE
    hp = HEADS_PER_STEP
    q_spec = pl.BlockSpec((1, hp, t, LANES), lambda b, g, i: (b, g, i, 0))
    k_spec = pl.BlockSpec((1, hp, S, LANES), lambda b, g, i: (b, g, 0, 0),
                          pipeline_mode=pl.Buffered(1))
    vt_all = pl.BlockSpec((1, hp, S // t, LANES, t), lambda b, g, i: (b, g, 0, 0, 0),
                          pipeline_mode=pl.Buffered(1))
    o_spec = pl.BlockSpec((1, t, hp * HEAD_DIM), lambda b, g, i: (b, i, g))
    att_shape = jax.ShapeDtypeStruct((B, S, MOBA_WIDTH), BF16)
    att_grid = (B, HEADS // hp, S // t)
    stats = [pltpu.VMEM((hp, 1, t), F32), pltpu.VMEM((hp, LANES, t), F32),
             pltpu.VMEM((hp, t, t), F32), pltpu.VMEM((hp, t, t), F32),
             pltpu.VMEM((hp, 1, t), F32), pltpu.VMEM((hp, 1, t), F32)]
    oa = pl.pallas_call(
        _moba_kernel, grid=att_grid, in_specs=[q_spec, k_spec, vt_all], out_specs=o_spec,
        out_shape=att_shape,
        scratch_shapes=stats + [pltpu.VMEM((hp, LANES, LANES), BF16),
                                pltpu.VMEM((hp, t, LANES), BF16)],
        compiler_params=_params(3), name="moba",
    )(qa, ka, vat)
    ob = pl.pallas_call(
        _mla_kernel, grid=att_grid, in_specs=[q_spec, k_spec, vt_all], out_specs=o_spec,
        out_shape=att_shape, scratch_shapes=stats, compiler_params=_params(3), name="mla",
    )(qm, km, vmt)

    tp = POST_TILE
    w_g = w_gate_up[:, :D_FF].reshape(D_MODEL, N_FF_CHUNKS, FF_CHUNK).transpose(1, 0, 2).astype(BF16)
    w_u = w_gate_up[:, D_FF:].reshape(D_MODEL, N_FF_CHUNKS, FF_CHUNK).transpose(1, 0, 2).astype(BF16)
    w_d = w_down.reshape(N_FF_CHUNKS, FF_CHUNK, D_MODEL).astype(BF16)
    ptok = lambda w: pl.BlockSpec((1, tp, w), lambda b, t: (b, t, 0))
    weights = [w_moba_branch.astype(BF16), w_mla_branch.astype(BF16), w_o.astype(BF16),
               row(ffn_norm), w_g, w_u, w_d, row(ple_norm), w_ple_gate.astype(BF16),
               w_ple_proj.astype(BF16), row(out_norm)]
    return pl.pallas_call(
        _post_kernel,
        grid=(B, S // tp),
        in_specs=[ptok(D_MODEL), ptok(MOBA_WIDTH), ptok(MOBA_WIDTH), ptok(D_MODEL), ptok(D_MODEL),
                  ptok(PLE_DIM)] + [_const_spec(w.shape) for w in weights],
        out_specs=ptok(D_MODEL),
        out_shape=jax.ShapeDtypeStruct((B, S, D_MODEL), F32),
        compiler_params=_params(2),
        name="post",
    )(x, oa, ob, sga, sgb, p, *weights)


def kernel(x, p, attn_norm, w_in, mla_q_norm, w_q_b, mla_kv_norm, w_kv_b, w_moba_branch,
           w_mla_branch, w_o, ffn_norm, w_gate_up, w_down, ple_norm, w_ple_gate, w_ple_proj,
           final_norm):
    depth = p.shape[0]
    assert depth == 1, "the final norm is fused into the (single) layer's post kernel"
    tables = _rope_tables(x.shape[1])
    return _layer(x, p[0], attn_norm[0], w_in[0], mla_q_norm[0], w_q_b[0], mla_kv_norm[0],
                  w_kv_b[0], w_moba_branch[0], w_mla_branch[0], w_o[0], ffn_norm[0],
                  w_gate_up[0], w_down[0], ple_norm[0], w_ple_gate[0], w_ple_proj[0],
                  final_norm, tables)
```

```python
import jax
import jax.numpy as jnp
from jax import lax
from jax.experimental import pallas as pl
from jax.experimental.pallas import tpu as pltpu

F32 = jnp.float32
BF16 = jnp.bfloat16

D_MODEL = 1024
PLE_DIM = 256
ROPE_THETA = 10000.0
EPS = 1e-6
NEG_INF = -1e30
LOG2_E = 1.4426950408889634

HEADS = 8
HEAD_DIM = 64
MOBA_WIDTH = HEADS * HEAD_DIM
MOBA_BLOCK = 256
MOBA_TOPK = 3
MLA_Q_RANK = 256
MLA_KV_RANK = 128
MLA_ROPE_DIM = 32
MLA_QK_DIM = HEAD_DIM + MLA_ROPE_DIM
D_FF = 2816

LANES = 128
AUX_LANE = HEAD_DIM
FF_CHUNK = 256
N_FF_CHUNKS = D_FF // FF_CHUNK
assert N_FF_CHUNKS * FF_CHUNK == D_FF

ATT_TILE = 512
PROJ_TILE = ATT_TILE
POST_TILE = 512
HEADS_PER_STEP = 4
VMEM_LIMIT = 56 * 1024 * 1024


def _lane_iota(shape):
    return lax.broadcasted_iota(jnp.int32, shape, len(shape) - 1)


def _rms(x, g):
    return x * lax.rsqrt(jnp.mean(x * x, axis=-1, keepdims=True) + EPS) * g


def _dot(a, b):
    return jnp.dot(a, b, preferred_element_type=F32)


def _dot_nt(a, b):
    return lax.dot_general(a, b, (((1,), (1,)), ((), ())), preferred_element_type=F32)


def _proj_kernel(x_ref, g_ref, win_ref, wvt_ref, gq_ref, wqb_ref, gkv_ref, wkb_ref, wvbt_ref,
                 cosa_ref, sina_ref, cosb_ref, sinb_ref,
                 qa_ref, ka_ref, vat_ref, qm_ref, km_ref, vmt_ref, sga_ref, sgb_ref):
    tm = x_ref.shape[1]
    h = _rms(x_ref[0], g_ref[...]).astype(BF16)
    lane = _lane_iota((tm, LANES))
    low_half = lane < HEAD_DIM
    row = lax.broadcasted_iota(jnp.int32, (tm, LANES), 0)
    blk = pl.program_id(1) * (tm // MOBA_BLOCK) + row // MOBA_BLOCK
    blk_onehot = jnp.where(lane == AUX_LANE + blk, 1.0, 0.0).astype(F32)
    aux_rows = jnp.where(lax.broadcasted_iota(jnp.int32, (LANES - HEAD_DIM, tm), 0) == 0,
                         1.0, 0.0).astype(BF16)

    def v_tile(vt_all, hd):
        return jnp.concatenate(
            [vt_all[hd * HEAD_DIM:(hd + 1) * HEAD_DIM].astype(BF16), aux_rows], axis=0)

    cosa, sina = cosa_ref[...], sina_ref[...]
    first_a = (lane & (HEAD_DIM // 2)) == 0

    def rope_a(t):
        partner = jnp.where(first_a, pltpu.roll(t, LANES - HEAD_DIM // 2, 1),
                            pltpu.roll(t, HEAD_DIM // 2, 1))
        return t * cosa + partner * sina

    def moba_part(col0, out_ref, scale, extra):
        t_all = _dot(h, win_ref[:, col0:col0 + MOBA_WIDTH])
        for g in range(HEADS // 2):
            t = rope_a(t_all[:, g * LANES:(g + 1) * LANES])
            if scale != 1.0:
                t = t * scale
            out_ref[0, 2 * g] = (jnp.where(low_half, t, 0.0) + extra).astype(BF16)
            out_ref[0, 2 * g + 1] = (jnp.where(low_half, pltpu.roll(t, HEAD_DIM, 1), 0.0)
                                     + extra).astype(BF16)

    moba_part(0, qa_ref, HEAD_DIM ** -0.5 * LOG2_E, 0.0)
    moba_part(MOBA_WIDTH, ka_ref, 1.0, blk_onehot)
    vat = _dot_nt(wvt_ref[...], h)
    for hd in range(HEADS):
        vat_ref[0, hd, 0] = v_tile(vat, hd)

    lat_col = 2 * MOBA_WIDTH
    lat = _dot(h, win_ref[:, lat_col:lat_col + 512])
    cosb, sinb = cosb_ref[...], sinb_ref[...]
    first_b = lane < AUX_LANE + MLA_ROPE_DIM // 2

    def rope_b(t):
        partner = jnp.where(first_b, pltpu.roll(t, LANES - MLA_ROPE_DIM // 2, 1),
                            pltpu.roll(t, MLA_ROPE_DIM // 2, 1))
        return t * cosb + partner * sinb

    cq = _rms(lat[:, :MLA_Q_RANK], gq_ref[...]).astype(BF16)
    qm = _dot(cq, wqb_ref[...])
    for hd in range(HEADS):
        t = rope_b(qm[:, hd * LANES:(hd + 1) * LANES]) * (MLA_QK_DIM ** -0.5 * LOG2_E)
        qm_ref[0, hd] = t.astype(BF16)

    ckv = _rms(lat[:, MLA_Q_RANK:MLA_Q_RANK + MLA_KV_RANK], gkv_ref[...]).astype(BF16)
    kr = rope_b(pltpu.roll(lat[:, MLA_Q_RANK + MLA_KV_RANK:], AUX_LANE, 1))
    kn = _dot(ckv, wkb_ref[...])
    vmt = _dot_nt(wvbt_ref[...], ckv)
    for hd in range(HEADS):
        km_ref[0, hd] = (kn[:, hd * LANES:(hd + 1) * LANES] + kr).astype(BF16)
        vmt_ref[0, hd, 0] = v_tile(vmt, hd)

    gate_col = lat_col + 512
    sga_ref[0] = jax.nn.sigmoid(_dot(h, win_ref[:, gate_col:gate_col + D_MODEL])).astype(BF16)
    sgb_ref[0] = jax.nn.sigmoid(
        _dot(h, win_ref[:, gate_col + D_MODEL:gate_col + 2 * D_MODEL])).astype(BF16)


def _attend(q_refs, k_ref, vt_ref, o_ref, m_ref, acc_ref, buf_a, buf_b):
    i = pl.program_id(2)
    t = ATT_TILE
    n_heads = len(q_refs)
    for hh in range(n_heads):
        m_ref[hh] = jnp.full((1, t), NEG_INF, F32)
        acc_ref[hh] = jnp.zeros((LANES, t), F32)

    def scores(j, buf, causal=False):
        s_ref, mx_ref = buf
        off = pl.multiple_of(j * t, t)
        for hh in range(n_heads):
            s = _dot_nt(k_ref[0, hh, pl.ds(off, t), :], q_refs[hh][...])
            if causal:
                key = lax.broadcasted_iota(jnp.int32, (t, t), 0)
                qry = lax.broadcasted_iota(jnp.int32, (t, t), 1)
                s = jnp.where(key <= qry, s, NEG_INF)
            s_ref[hh] = s
            mx_ref[hh] = jnp.max(s, axis=0, keepdims=True)

    def softmax_pv(j, buf):
        s_ref, mx_ref = buf
        for hh in range(n_heads):
            m_old = m_ref[hh]
            m_new = jnp.maximum(m_old, mx_ref[hh])
            p = jnp.exp2(s_ref[hh] - m_new).astype(BF16)
            acc_ref[hh] = jnp.exp2(m_old - m_new) * acc_ref[hh] + _dot(vt_ref[0, hh, j], p)
            m_ref[hh] = m_new

    @pl.when(i == 0)
    def _():
        scores(0, buf_a, causal=True)
        softmax_pv(0, buf_a)

    @pl.when(i > 0)
    def _():
        scores(0, buf_a)

        def body(jj, carry):
            j = 2 * jj
            scores(j + 1, buf_b)
            softmax_pv(j, buf_a)
            scores(j + 2, buf_a)
            softmax_pv(j + 1, buf_b)
            return carry

        lax.fori_loop(0, (i - 1) // 2, body, 0)

        @pl.when(i % 2 == 1)
        def _():
            scores(i, buf_b, causal=True)
            softmax_pv(i - 1, buf_a)
            softmax_pv(i, buf_b)

        @pl.when(i % 2 == 0)
        def _():
            scores(i - 1, buf_b)
            softmax_pv(i - 2, buf_a)
            scores(i, buf_a, causal=True)
            softmax_pv(i - 1, buf_b)
            softmax_pv(i, buf_a)

    outs = []
    for hh in range(n_heads):
        acc = acc_ref[hh]
        outs.append(acc[:HEAD_DIM] / acc[AUX_LANE:AUX_LANE + 1])
    o_ref[0] = jnp.concatenate(outs, axis=0).T.astype(o_ref.dtype)


def _mla_kernel(q_ref, k_ref, vt_ref, o_ref, m_ref, acc_ref, sa_ref, sb_ref, mxa_ref, mxb_ref):
    _attend([q_ref.at[0, hh] for hh in range(HEADS_PER_STEP)], k_ref, vt_ref, o_ref,
            m_ref, acc_ref, (sa_ref, mxa_ref), (sb_ref, mxb_ref))


def _moba_kernel(q_ref, k_ref, vt_ref, o_ref, m_ref, acc_ref, sa_ref, sb_ref, mxa_ref, mxb_ref,
                 kmean_ref, qaug_ref):
    i = pl.program_id(2)
    t = ATT_TILE
    n_blocks = k_ref.shape[2] // MOBA_BLOCK

    @pl.when(i == 0)
    def _():
        kmean_ref[...] = jnp.zeros(kmean_ref.shape, kmean_ref.dtype)
        for hh in range(HEADS_PER_STEP):
            kb = k_ref[0, hh].astype(F32).reshape(n_blocks, MOBA_BLOCK, LANES)
            kmean_ref[hh, AUX_LANE:AUX_LANE + n_blocks, :] = (
                jnp.sum(kb, axis=1) * (1.0 / MOBA_BLOCK)).astype(kmean_ref.dtype)

    blk = lax.broadcasted_iota(jnp.int32, (n_blocks, t), 0)
    blk_f = blk.astype(F32)
    qcol = lax.broadcasted_iota(jnp.int32, (n_blocks, t), 1)
    cur = i * (t // MOBA_BLOCK) + qcol // MOBA_BLOCK
    past = blk < cur
    for hh in range(HEADS_PER_STEP):
        q = q_ref[0, hh]
        gate = _dot_nt(kmean_ref[hh], q)[AUX_LANE:AUX_LANE + n_blocks]
        gate = jnp.where(past, gate, NEG_INF)
        chosen = blk == cur
        for _ in range(MOBA_TOPK):
            best = jnp.max(gate, axis=0, keepdims=True)
            first = jnp.min(jnp.where(gate == best, blk_f, float(n_blocks)), axis=0,
                            keepdims=True)
            pick = (blk_f == first) & (best > NEG_INF * 0.5)
            chosen = chosen | pick
            gate = jnp.where(pick, NEG_INF, gate)
        bias = jnp.concatenate(
            [jnp.zeros((AUX_LANE, t), F32), jnp.where(chosen, 0.0, NEG_INF),
             jnp.zeros((LANES - AUX_LANE - n_blocks, t), F32)], axis=0).T
        qaug_ref[hh] = (q.astype(F32) + bias).astype(BF16)
    _attend([qaug_ref.at[hh] for hh in range(HEADS_PER_STEP)], k_ref, vt_ref, o_ref,
            m_ref, acc_ref, (sa_ref, mxa_ref), (sb_ref, mxb_ref))


def _post_kernel(x_ref, oa_ref, ob_ref, sga_ref, sgb_ref, p_ref,
                 wa_ref, wb_ref, wo_ref, gf_ref, wgu_ref, wd_ref,
                 gp_ref, wpg_ref, wpp_ref, gn_ref, out_ref):
    x = x_ref[0]
    ya = _dot(oa_ref[0], wa_ref[...])
    yb = _dot(ob_ref[0], wb_ref[...])
    mix = sga_ref[0].astype(F32) * ya + sgb_ref[0].astype(F32) * yb
    x = x + _dot(mix.astype(BF16), wo_ref[...])

    h = _rms(x, gf_ref[...]).astype(BF16)

    def gate_up(c):
        cols = slice(c * FF_CHUNK, (c + 1) * FF_CHUNK)
        return _dot(h, wgu_ref[:, cols]), _dot(h, wgu_ref[:, D_FF + c * FF_CHUNK:
                                                          D_FF + (c + 1) * FF_CHUNK])

    g, u = gate_up(0)
    ffn = None
    for c in range(N_FF_CHUNKS):
        a = (g * jax.nn.sigmoid(g) * u).astype(BF16)
        if c + 1 < N_FF_CHUNKS:
            g, u = gate_up(c + 1)
        d = _dot(a, wd_ref[c * FF_CHUNK:(c + 1) * FF_CHUNK, :])
        ffn = d if ffn is None else ffn + d
    x = x + ffn

    h = _rms(x, gp_ref[...]).astype(BF16)
    x = x + jax.nn.sigmoid(_dot(h, wpg_ref[...])) * _dot(p_ref[0].astype(BF16), wpp_ref[...])
    out_ref[0] = _rms(x, gn_ref[...])


def _rope_tables(seq):
    pos = jnp.arange(seq, dtype=F32)[:, None]

    def cos_sin(d):
        half = d // 2
        inv_freq = 1.0 / (ROPE_THETA ** (jnp.arange(half, dtype=F32) * (2.0 / d)))
        ang = pos * inv_freq[None, :]
        return jnp.cos(ang), jnp.sin(ang)

    ca, sa = cos_sin(HEAD_DIM)
    cosa = jnp.concatenate([ca, ca, ca, ca], axis=1)
    sina = jnp.concatenate([-sa, sa, -sa, sa], axis=1)
    cb, sb = cos_sin(MLA_ROPE_DIM)
    one = jnp.ones((seq, HEAD_DIM), F32)
    cosb = jnp.concatenate([one, cb, cb, one[:, :LANES - MLA_QK_DIM]], axis=1)
    sinb = jnp.concatenate([0 * one, -sb, sb, 0 * one[:, :LANES - MLA_QK_DIM]], axis=1)
    return cosa, sina, cosb, sinb


def _pad_heads(w, per_head, lo, hi):
    k = w.shape[0]
    w = w.reshape(k, HEADS, per_head)[:, :, lo:hi]
    return jnp.pad(w, ((0, 0), (0, 0), (0, LANES - (hi - lo)))).reshape(k, HEADS * LANES)


def _const_spec(shape):
    zeros = (0,) * len(shape)
    return pl.BlockSpec(shape, lambda *_: zeros, pipeline_mode=pl.Buffered(1))


def _params(n_axes):
    return pltpu.CompilerParams(dimension_semantics=("arbitrary",) * n_axes,
                                vmem_limit_bytes=VMEM_LIMIT)


def _layer(x, p, attn_norm, w_in, mla_q_norm, w_q_b, mla_kv_norm, w_kv_b,
           w_moba_branch, w_mla_branch, w_o, ffn_norm, w_gate_up, w_down,
           ple_norm, w_ple_gate, w_ple_proj, out_norm, tables):
    B, S, _ = x.shape
    cosa, sina, cosb, sinb = tables

    v_lo, v_hi = 2 * MOBA_WIDTH, 3 * MOBA_WIDTH
    kr_end = v_hi + MLA_Q_RANK + MLA_KV_RANK + MLA_ROPE_DIM
    w_in_p = jnp.concatenate(
        [w_in[:, :v_lo], w_in[:, v_hi:kr_end],
         jnp.zeros((D_MODEL, LANES - MLA_ROPE_DIM), w_in.dtype), w_in[:, kr_end:]],
        axis=1).astype(BF16)
    w_vt_p = w_in[:, v_lo:v_hi].T.astype(BF16)
    w_qb_p = _pad_heads(w_q_b, MLA_QK_DIM, 0, MLA_QK_DIM).astype(BF16)
    w_kb_p = _pad_heads(w_kv_b, 2 * HEAD_DIM, 0, HEAD_DIM).astype(BF16)
    w_vbt_p = w_kv_b.reshape(MLA_KV_RANK, HEADS, 2 * HEAD_DIM)[:, :, HEAD_DIM:].reshape(
        MLA_KV_RANK, MOBA_WIDTH).T.astype(BF16)
    row = lambda g: g.reshape(1, -1).astype(F32)

    tm = PROJ_TILE
    head_shape = jax.ShapeDtypeStruct((B, HEADS, S, LANES), BF16)
    vt_shape = jax.ShapeDtypeStruct((B, HEADS, S // tm, LANES, tm), BF16)
    gate_shape = jax.ShapeDtypeStruct((B, S, D_MODEL), BF16)
    head_spec = pl.BlockSpec((1, HEADS, tm, LANES), lambda b, t: (b, 0, t, 0))
    vt_spec = pl.BlockSpec((1, HEADS, 1, LANES, tm), lambda b, t: (b, 0, t, 0, 0))
    tok_spec = lambda w: pl.BlockSpec((1, tm, w), lambda b, t: (b, t, 0))
    tab_spec = pl.BlockSpec((tm, LANES), lambda b, t: (t, 0))
    qa, ka, vat, qm, km, vmt, sga, sgb = pl.pallas_call(
        _proj_kernel,
        grid=(B, S // tm),
        in_specs=[tok_spec(D_MODEL), _const_spec((1, D_MODEL)), _const_spec(w_in_p.shape),
                  _const_spec(w_vt_p.shape),
                  _const_spec((1, MLA_Q_RANK)), _const_spec(w_qb_p.shape),
                  _const_spec((1, MLA_KV_RANK)), _const_spec(w_kb_p.shape),
                  _const_spec(w_vbt_p.shape),
                  tab_spec, tab_spec, tab_spec, tab_spec],
        out_specs=[head_spec, head_spec, vt_spec, head_spec, head_spec, vt_spec,
                   tok_spec(D_MODEL), tok_spec(D_MODEL)],
        out_shape=[head_shape, head_shape, vt_shape, head_shape, head_shape, vt_shape,
                   gate_shape, gate_shape],
        compiler_params=_params(2),
        name="proj",
    )(x, row(attn_norm), w_in_p, w_vt_p, row(mla_q_norm), w_qb_p, row(mla_kv_norm), w_kb_p,
      w_vbt_p, cosa, sina, cosb, sinb)

    t = ATT_TILE
    hp = HEADS_PER_STEP
    q_spec = pl.BlockSpec((1, hp, t, LANES), lambda b, g, i: (b, g, i, 0))
    k_spec = pl.BlockSpec((1, hp, S, LANES), lambda b, g, i: (b, g, 0, 0),
                          pipeline_mode=pl.Buffered(1))
    vt_all = pl.BlockSpec((1, hp, S // t, LANES, t), lambda b, g, i: (b, g, 0, 0, 0),
                          pipeline_mode=pl.Buffered(1))
    o_spec = pl.BlockSpec((1, t, hp * HEAD_DIM), lambda b, g, i: (b, i, g))
    att_shape = jax.ShapeDtypeStruct((B, S, MOBA_WIDTH), BF16)
    att_grid = (B, HEADS // hp, S // t)
    stats = [pltpu.VMEM((hp, 1, t), F32), pltpu.VMEM((hp, LANES, t), F32),
             pltpu.VMEM((hp, t, t), F32), pltpu.VMEM((hp, t, t), F32),
             pltpu.VMEM((hp, 1, t), F32), pltpu.VMEM((hp, 1, t), F32)]
    oa = pl.pallas_call(
        _moba_kernel, grid=att_grid, in_specs=[q_spec, k_spec, vt_all], out_specs=o_spec,
        out_shape=att_shape,
        scratch_shapes=stats + [pltpu.VMEM((hp, LANES, LANES), BF16),
                                pltpu.VMEM((hp, t, LANES), BF16)],
        compiler_params=_params(3), name="moba",
    )(qa, ka, vat)
    ob = pl.pallas_call(
        _mla_kernel, grid=att_grid, in_specs=[q_spec, k_spec, vt_all], out_specs=o_spec,
        out_shape=att_shape, scratch_shapes=stats, compiler_params=_params(3), name="mla",
    )(qm, km, vmt)

    tp = POST_TILE
    ptok = lambda w: pl.BlockSpec((1, tp, w), lambda b, t: (b, t, 0))
    weights = [w_moba_branch.astype(BF16), w_mla_branch.astype(BF16), w_o.astype(BF16),
               row(ffn_norm), w_gate_up.astype(BF16), w_down.astype(BF16),
               row(ple_norm), w_ple_gate.astype(BF16),
               w_ple_proj.astype(BF16), row(out_norm)]
    return pl.pallas_call(
        _post_kernel,
        grid=(B, S // tp),
        in_specs=[ptok(D_MODEL), ptok(MOBA_WIDTH), ptok(MOBA_WIDTH), ptok(D_MODEL), ptok(D_MODEL),
                  ptok(PLE_DIM)] + [_const_spec(w.shape) for w in weights],
        out_specs=ptok(D_MODEL),
        out_shape=jax.ShapeDtypeStruct((B, S, D_MODEL), F32),
        compiler_params=_params(2),
        name="post",
    )(x, oa, ob, sga, sgb, p, *weights)


def kernel(x, p, attn_norm, w_in, mla_q_norm, w_q_b, mla_kv_norm, w_kv_b, w_moba_branch,
           w_mla_branch, w_o, ffn_norm, w_gate_up, w_down, ple_norm, w_ple_gate, w_ple_proj,
           final_norm):
    depth = p.shape[0]
    assert depth == 1, "the final norm is fused into the (single) layer's post kernel"
    tables = _rope_tables(x.shape[1])
    return _layer(x, p[0], attn_norm[0], w_in[0], mla_q_norm[0], w_q_b[0], mla_kv_norm[0],
                  w_kv_b[0], w_moba_branch[0], w_mla_branch[0], w_o[0], ffn_norm[0],
                  w_gate_up[0], w_down[0], ple_norm[0], w_ple_gate[0], w_ple_proj[0],
                  final_norm, tables)
```

```python
import jax
import jax.numpy as jnp
from jax import lax
from jax.experimental import pallas as pl
from jax.experimental.pallas import tpu as pltpu

F32 = jnp.float32
BF16 = jnp.bfloat16

D_MODEL = 1024
PLE_DIM = 256
ROPE_THETA = 10000.0
EPS = 1e-6
NEG_INF = -1e30
LOG2_E = 1.4426950408889634

HEADS = 8
HEAD_DIM = 64
MOBA_WIDTH = HEADS * HEAD_DIM
MOBA_BLOCK = 256
MOBA_TOPK = 3
MLA_Q_RANK = 256
MLA_KV_RANK = 128
MLA_ROPE_DIM = 32
MLA_QK_DIM = HEAD_DIM + MLA_ROPE_DIM
D_FF = 2816

LANES = 128
AUX_LANE = HEAD_DIM
FF_CHUNK = 256
N_FF_CHUNKS = D_FF // FF_CHUNK
assert N_FF_CHUNKS * FF_CHUNK == D_FF

ATT_TILE = 512
PROJ_TILE = ATT_TILE
POST_TILE = 512
HEADS_PER_STEP = 4
VMEM_LIMIT = 56 * 1024 * 1024


def _lane_iota(shape):
    return lax.broadcasted_iota(jnp.int32, shape, len(shape) - 1)


def _rms(x, g):
    return x * lax.rsqrt(jnp.mean(x * x, axis=-1, keepdims=True) + EPS) * g


def _dot(a, b):
    return jnp.dot(a, b, preferred_element_type=F32)


def _dot_nt(a, b):
    return lax.dot_general(a, b, (((1,), (1,)), ((), ())), preferred_element_type=F32)


def _proj_kernel(x_ref, g_ref, win_ref, wvt_ref, gq_ref, wqb_ref, gkv_ref, wkb_ref, wvbt_ref,
                 cosa_ref, sina_ref, cosb_ref, sinb_ref,
                 qa_ref, ka_ref, vat_ref, qm_ref, km_ref, vmt_ref, sga_ref, sgb_ref):
    tm = x_ref.shape[1]
    h = _rms(x_ref[0], g_ref[...]).astype(BF16)
    lane = _lane_iota((tm, LANES))
    low_half = lane < HEAD_DIM
    row = lax.broadcasted_iota(jnp.int32, (tm, LANES), 0)
    blk = pl.program_id(1) * (tm // MOBA_BLOCK) + row // MOBA_BLOCK
    blk_onehot = jnp.where(lane == AUX_LANE + blk, 1.0, 0.0).astype(F32)
    aux_rows = jnp.where(lax.broadcasted_iota(jnp.int32, (LANES - HEAD_DIM, tm), 0) == 0,
                         1.0, 0.0).astype(BF16)

    def v_tile(vt_all, hd):
        return jnp.concatenate(
            [vt_all[hd * HEAD_DIM:(hd + 1) * HEAD_DIM].astype(BF16), aux_rows], axis=0)

    cosa, sina = cosa_ref[...], sina_ref[...]
    first_a = (lane & (HEAD_DIM // 2)) == 0

    def rope_a(t):
        partner = jnp.where(first_a, pltpu.roll(t, LANES - HEAD_DIM // 2, 1),
                            pltpu.roll(t, HEAD_DIM // 2, 1))
        return t * cosa + partner * sina

    def moba_part(col0, out_ref, scale, extra):
        t_all = _dot(h, win_ref[:, col0:col0 + MOBA_WIDTH])
        for g in range(HEADS // 2):
            t = rope_a(t_all[:, g * LANES:(g + 1) * LANES])
            if scale != 1.0:
                t = t * scale
            out_ref[0, 2 * g] = (jnp.where(low_half, t, 0.0) + extra).astype(BF16)
            out_ref[0, 2 * g + 1] = (jnp.where(low_half, pltpu.roll(t, HEAD_DIM, 1), 0.0)
                                     + extra).astype(BF16)

    moba_part(0, qa_ref, HEAD_DIM ** -0.5 * LOG2_E, 0.0)
    moba_part(MOBA_WIDTH, ka_ref, 1.0, blk_onehot)
    vat = _dot_nt(wvt_ref[...], h)
    for hd in range(HEADS):
        vat_ref[0, hd, 0] = v_tile(vat, hd)

    lat_col = 2 * MOBA_WIDTH
    lat = _dot(h, win_ref[:, lat_col:lat_col + 512])
    cosb, sinb = cosb_ref[...], sinb_ref[...]
    first_b = lane < AUX_LANE + MLA_ROPE_DIM // 2

    def rope_b(t):
        partner = jnp.where(first_b, pltpu.roll(t, LANES - MLA_ROPE_DIM // 2, 1),
                            pltpu.roll(t, MLA_ROPE_DIM // 2, 1))
        return t * cosb + partner * sinb

    cq = _rms(lat[:, :MLA_Q_RANK], gq_ref[...]).astype(BF16)
    qm = _dot(cq, wqb_ref[...])
    for hd in range(HEADS):
        t = rope_b(qm[:, hd * LANES:(hd + 1) * LANES]) * (MLA_QK_DIM ** -0.5 * LOG2_E)
        qm_ref[0, hd] = t.astype(BF16)

    ckv = _rms(lat[:, MLA_Q_RANK:MLA_Q_RANK + MLA_KV_RANK], gkv_ref[...]).astype(BF16)
    kr = rope_b(pltpu.roll(lat[:, MLA_Q_RANK + MLA_KV_RANK:], AUX_LANE, 1))
    kn = _dot(ckv, wkb_ref[...])
    vmt = _dot_nt(wvbt_ref[...], ckv)
    for hd in range(HEADS):
        km_ref[0, hd] = (kn[:, hd * LANES:(hd + 1) * LANES] + kr).astype(BF16)
        vmt_ref[0, hd, 0] = v_tile(vmt, hd)

    gate_col = lat_col + 512
    sga_ref[0] = jax.nn.sigmoid(_dot(h, win_ref[:, gate_col:gate_col + D_MODEL])).astype(BF16)
    sgb_ref[0] = jax.nn.sigmoid(
        _dot(h, win_ref[:, gate_col + D_MODEL:gate_col + 2 * D_MODEL])).astype(BF16)


def _attend(q_refs, k_ref, vt_ref, o_ref, m_ref, acc_ref, buf_a, buf_b):
    i = pl.program_id(2)
    t = ATT_TILE
    n_heads = len(q_refs)
    for hh in range(n_heads):
        m_ref[hh] = jnp.full((1, t), NEG_INF, F32)
        acc_ref[hh] = jnp.zeros((LANES, t), F32)

    def scores_head(hh, j, buf, causal):
        s_ref, mx_ref = buf
        off = pl.multiple_of(j * t, t)
        s = _dot_nt(k_ref[0, hh, pl.ds(off, t), :], q_refs[hh][...])
        if causal:
            key = lax.broadcasted_iota(jnp.int32, (t, t), 0)
            qry = lax.broadcasted_iota(jnp.int32, (t, t), 1)
            s = jnp.where(key <= qry, s, NEG_INF)
        s_ref[hh] = s
        mx_ref[hh] = jnp.max(s, axis=0, keepdims=True)

    def softmax_pv_head(hh, j, buf):
        s_ref, mx_ref = buf
        m_old = m_ref[hh]
        m_new = jnp.maximum(m_old, mx_ref[hh])
        p = jnp.exp2(s_ref[hh] - m_new).astype(BF16)
        acc_ref[hh] = jnp.exp2(m_old - m_new) * acc_ref[hh] + _dot(vt_ref[0, hh, j], p)
        m_ref[hh] = m_new

    def scores(j, buf, causal=False):
        for hh in range(n_heads):
            scores_head(hh, j, buf, causal)

    def softmax_pv(j, buf):
        for hh in range(n_heads):
            softmax_pv_head(hh, j, buf)

    def stage(j_next, buf_next, j_cur, buf_cur, causal=False):
        for hh in range(n_heads):
            scores_head(hh, j_next, buf_next, causal)
            softmax_pv_head(hh, j_cur, buf_cur)

    @pl.when(i == 0)
    def _():
        scores(0, buf_a, causal=True)
        softmax_pv(0, buf_a)

    @pl.when(i > 0)
    def _():
        scores(0, buf_a)

        def body(jj, carry):
            j = 2 * jj
            stage(j + 1, buf_b, j, buf_a)
            stage(j + 2, buf_a, j + 1, buf_b)
            return carry

        lax.fori_loop(0, (i - 1) // 2, body, 0)

        @pl.when(i % 2 == 1)
        def _():
            stage(i, buf_b, i - 1, buf_a, causal=True)
            softmax_pv(i, buf_b)

        @pl.when(i % 2 == 0)
        def _():
            stage(i - 1, buf_b, i - 2, buf_a)
            stage(i, buf_a, i - 1, buf_b, causal=True)
            softmax_pv(i, buf_a)

    outs = []
    for hh in range(n_heads):
        acc = acc_ref[hh]
        outs.append(acc[:HEAD_DIM] / acc[AUX_LANE:AUX_LANE + 1])
    o_ref[0] = jnp.concatenate(outs, axis=0).T.astype(o_ref.dtype)


def _mla_kernel(q_ref, k_ref, vt_ref, o_ref, m_ref, acc_ref, sa_ref, sb_ref, mxa_ref, mxb_ref):
    _attend([q_ref.at[0, hh] for hh in range(HEADS_PER_STEP)], k_ref, vt_ref, o_ref,
            m_ref, acc_ref, (sa_ref, mxa_ref), (sb_ref, mxb_ref))


def _moba_kernel(q_ref, k_ref, vt_ref, o_ref, m_ref, acc_ref, sa_ref, sb_ref, mxa_ref, mxb_ref,
                 kmean_ref, qaug_ref):
    i = pl.program_id(2)
    t = ATT_TILE
    n_blocks = k_ref.shape[2] // MOBA_BLOCK

    @pl.when(i == 0)
    def _():
        kmean_ref[...] = jnp.zeros(kmean_ref.shape, kmean_ref.dtype)
        for hh in range(HEADS_PER_STEP):
            kb = k_ref[0, hh].astype(F32).reshape(n_blocks, MOBA_BLOCK, LANES)
            kmean_ref[hh, AUX_LANE:AUX_LANE + n_blocks, :] = (
                jnp.sum(kb, axis=1) * (1.0 / MOBA_BLOCK)).astype(kmean_ref.dtype)

    blk = lax.broadcasted_iota(jnp.int32, (n_blocks, t), 0)
    blk_f = blk.astype(F32)
    qcol = lax.broadcasted_iota(jnp.int32, (n_blocks, t), 1)
    cur = i * (t // MOBA_BLOCK) + qcol // MOBA_BLOCK
    past = blk < cur
    for hh in range(HEADS_PER_STEP):
        q = q_ref[0, hh]
        gate = _dot_nt(kmean_ref[hh], q)[AUX_LANE:AUX_LANE + n_blocks]
        gate = jnp.where(past, gate, NEG_INF)
        chosen = blk == cur
        for _ in range(MOBA_TOPK):
            best = jnp.max(gate, axis=0, keepdims=True)
            first = jnp.min(jnp.where(gate == best, blk_f, float(n_blocks)), axis=0,
                            keepdims=True)
            pick = (blk_f == first) & (best > NEG_INF * 0.5)
            chosen = chosen | pick
            gate = jnp.where(pick, NEG_INF, gate)
        bias = jnp.concatenate(
            [jnp.zeros((AUX_LANE, t), F32), jnp.where(chosen, 0.0, NEG_INF),
             jnp.zeros((LANES - AUX_LANE - n_blocks, t), F32)], axis=0).T
        qaug_ref[hh] = (q.astype(F32) + bias).astype(BF16)
    _attend([qaug_ref.at[hh] for hh in range(HEADS_PER_STEP)], k_ref, vt_ref, o_ref,
            m_ref, acc_ref, (sa_ref, mxa_ref), (sb_ref, mxb_ref))


def _post_kernel(x_ref, oa_ref, ob_ref, sga_ref, sgb_ref, p_ref,
                 wa_ref, wb_ref, wo_ref, gf_ref, wgu_ref, wd_ref,
                 gp_ref, wpg_ref, wpp_ref, gn_ref, out_ref):
    x = x_ref[0]
    ya = _dot(oa_ref[0], wa_ref[...])
    yb = _dot(ob_ref[0], wb_ref[...])
    mix = sga_ref[0].astype(F32) * ya + sgb_ref[0].astype(F32) * yb
    x = x + _dot(mix.astype(BF16), wo_ref[...])

    h = _rms(x, gf_ref[...]).astype(BF16)

    def gate_up(c):
        cols = slice(c * FF_CHUNK, (c + 1) * FF_CHUNK)
        return _dot(h, wgu_ref[:, cols]), _dot(h, wgu_ref[:, D_FF + c * FF_CHUNK:
                                                          D_FF + (c + 1) * FF_CHUNK])

    g, u = gate_up(0)
    ffn = None
    for c in range(N_FF_CHUNKS):
        a = (g * jax.nn.sigmoid(g) * u).astype(BF16)
        if c + 1 < N_FF_CHUNKS:
            g, u = gate_up(c + 1)
        d = _dot(a, wd_ref[c * FF_CHUNK:(c + 1) * FF_CHUNK, :])
        ffn = d if ffn is None else ffn + d
    x = x + ffn

    h = _rms(x, gp_ref[...]).astype(BF16)
    x = x + jax.nn.sigmoid(_dot(h, wpg_ref[...])) * _dot(p_ref[0].astype(BF16), wpp_ref[...])
    out_ref[0] = _rms(x, gn_ref[...])


def _rope_tables(seq):
    pos = jnp.arange(seq, dtype=F32)[:, None]

    def cos_sin(d):
        half = d // 2
        inv_freq = 1.0 / (ROPE_THETA ** (jnp.arange(half, dtype=F32) * (2.0 / d)))
        ang = pos * inv_freq[None, :]
        return jnp.cos(ang), jnp.sin(ang)

    ca, sa = cos_sin(HEAD_DIM)
    cosa = jnp.concatenate([ca, ca, ca, ca], axis=1)
    sina = jnp.concatenate([-sa, sa, -sa, sa], axis=1)
    cb, sb = cos_sin(MLA_ROPE_DIM)
    one = jnp.ones((seq, HEAD_DIM), F32)
    cosb = jnp.concatenate([one, cb, cb, one[:, :LANES - MLA_QK_DIM]], axis=1)
    sinb = jnp.concatenate([0 * one, -sb, sb, 0 * one[:, :LANES - MLA_QK_DIM]], axis=1)
    return cosa, sina, cosb, sinb


def _pad_heads(w, per_head, lo, hi):
    k = w.shape[0]
    w = w.reshape(k, HEADS, per_head)[:, :, lo:hi]
    return jnp.pad(w, ((0, 0), (0, 0), (0, LANES - (hi - lo)))).reshape(k, HEADS * LANES)


def _const_spec(shape):
    zeros = (0,) * len(shape)
    return pl.BlockSpec(shape, lambda *_: zeros, pipeline_mode=pl.Buffered(1))


def _params(n_axes):
    return pltpu.CompilerParams(dimension_semantics=("arbitrary",) * n_axes,
                                vmem_limit_bytes=VMEM_LIMIT)


def _layer(x, p, attn_norm, w_in, mla_q_norm, w_q_b, mla_kv_norm, w_kv_b,
           w_moba_branch, w_mla_branch, w_o, ffn_norm, w_gate_up, w_down,
           ple_norm, w_ple_gate, w_ple_proj, out_norm, tables):
    B, S, _ = x.shape
    cosa, sina, cosb, sinb = tables

    v_lo, v_hi = 2 * MOBA_WIDTH, 3 * MOBA_WIDTH
    kr_end = v_hi + MLA_Q_RANK + MLA_KV_RANK + MLA_ROPE_DIM
    w_in_p = jnp.concatenate(
        [w_in[:, :v_lo], w_in[:, v_hi:kr_end],
         jnp.zeros((D_MODEL, LANES - MLA_ROPE_DIM), w_in.dtype), w_in[:, kr_end:]],
        axis=1).astype(BF16)
    w_vt_p = w_in[:, v_lo:v_hi].T.astype(BF16)
    w_qb_p = _pad_heads(w_q_b, MLA_QK_DIM, 0, MLA_QK_DIM).astype(BF16)
    w_kb_p = _pad_heads(w_kv_b, 2 * HEAD_DIM, 0, HEAD_DIM).astype(BF16)
    w_vbt_p = w_kv_b.reshape(MLA_KV_RANK, HEADS, 2 * HEAD_DIM)[:, :, HEAD_DIM:].reshape(
        MLA_KV_RANK, MOBA_WIDTH).T.astype(BF16)
    row = lambda g: g.reshape(1, -1).astype(F32)

    tm = PROJ_TILE
    head_shape = jax.ShapeDtypeStruct((B, HEADS, S, LANES), BF16)
    vt_shape = jax.ShapeDtypeStruct((B, HEADS, S // tm, LANES, tm), BF16)
    gate_shape = jax.ShapeDtypeStruct((B, S, D_MODEL), BF16)
    head_spec = pl.BlockSpec((1, HEADS, tm, LANES), lambda b, t: (b, 0, t, 0))
    vt_spec = pl.BlockSpec((1, HEADS, 1, LANES, tm), lambda b, t: (b, 0, t, 0, 0))
    tok_spec = lambda w: pl.BlockSpec((1, tm, w), lambda b, t: (b, t, 0))
    tab_spec = pl.BlockSpec((tm, LANES), lambda b, t: (t, 0))
    qa, ka, vat, qm, km, vmt, sga, sgb = pl.pallas_call(
        _proj_kernel,
        grid=(B, S // tm),
        in_specs=[tok_spec(D_MODEL), _const_spec((1, D_MODEL)), _const_spec(w_in_p.shape),
                  _const_spec(w_vt_p.shape),
                  _const_spec((1, MLA_Q_RANK)), _const_spec(w_qb_p.shape),
                  _const_spec((1, MLA_KV_RANK)), _const_spec(w_kb_p.shape),
                  _const_spec(w_vbt_p.shape),
                  tab_spec, tab_spec, tab_spec, tab_spec],
        out_specs=[head_spec, head_spec, vt_spec, head_spec, head_spec, vt_spec,
                   tok_spec(D_MODEL), tok_spec(D_MODEL)],
        out_shape=[head_shape, head_shape, vt_shape, head_shape, head_shape, vt_shape,
                   gate_shape, gate_shape],
        compiler_params=_params(2),
        name="proj",
    )(x, row(attn_norm), w_in_p, w_vt_p, row(mla_q_norm), w_qb_p, row(mla_kv_norm), w_kb_p,
      w_vbt_p, cosa, sina, cosb, sinb)

    t = ATT_TILE
    hp = HEADS_PER_STEP
    q_spec = pl.BlockSpec((1, hp, t, LANES), lambda b, g, i: (b, g, i, 0))
    k_spec = pl.BlockSpec((1, hp, S, LANES), lambda b, g, i: (b, g, 0, 0),
                          pipeline_mode=pl.Buffered(1))
    vt_all = pl.BlockSpec((1, hp, S // t, LANES, t), lambda b, g, i: (b, g, 0, 0, 0),
                          pipeline_mode=pl.Buffered(1))
    o_spec = pl.BlockSpec((1, t, hp * HEAD_DIM), lambda b, g, i: (b, i, g))
    att_shape = jax.ShapeDtypeStruct((B, S, MOBA_WIDTH), BF16)
    att_grid = (B, HEADS // hp, S // t)
    stats = [pltpu.VMEM((hp, 1, t), F32), pltpu.VMEM((hp, LANES, t), F32),
             pltpu.VMEM((hp, t, t), F32), pltpu.VMEM((hp, t, t), F32),
             pltpu.VMEM((hp, 1, t), F32), pltpu.VMEM((hp, 1, t), F32)]
    oa = pl.pallas_call(
        _moba_kernel, grid=att_grid, in_specs=[q_spec, k_spec, vt_all], out_specs=o_spec,
        out_shape=att_shape,
        scratch_shapes=stats + [pltpu.VMEM((hp, LANES, LANES), BF16),
                                pltpu.VMEM((hp, t, LANES), BF16)],
        compiler_params=_params(3), name="moba",
    )(qa, ka, vat)
    ob = pl.pallas_call(
        _mla_kernel, grid=att_grid, in_specs=[q_spec, k_spec, vt_all], out_specs=o_spec,
        out_shape=att_shape, scratch_shapes=stats, compiler_params=_params(3), name="mla",
    )(qm, km, vmt)

    tp = POST_TILE
    ptok = lambda w: pl.BlockSpec((1, tp, w), lambda b, t: (b, t, 0))
    weights = [w_moba_branch.astype(BF16), w_mla_branch.astype(BF16), w_o.astype(BF16),
               row(ffn_norm), w_gate_up.astype(BF16), w_down.astype(BF16),
               row(ple_norm), w_ple_gate.astype(BF16),
               w_ple_proj.astype(BF16), row(out_norm)]
    return pl.pallas_call(
        _post_kernel,
        grid=(B, S // tp),
        in_specs=[ptok(D_MODEL), ptok(MOBA_WIDTH), ptok(MOBA_WIDTH), ptok(D_MODEL), ptok(D_MODEL),
                  ptok(PLE_DIM)] + [_const_spec(w.shape) for w in weights],
        out_specs=ptok(D_MODEL),
        out_shape=jax.ShapeDtypeStruct((B, S, D_MODEL), F32),
        compiler_params=_params(2),
        name="post",
    )(x, oa, ob, sga, sgb, p, *weights)


def kernel(x, p, attn_norm, w_in, mla_q_norm, w_q_b, mla_kv_norm, w_kv_b, w_moba_branch,
           w_mla_branch, w_o, ffn_norm, w_gate_up, w_down, ple_norm, w_ple_gate, w_ple_proj,
           final_norm):
    depth = p.shape[0]
    assert depth == 1, "the final norm is fused into the (single) layer's post kernel"
    tables = _rope_tables(x.shape[1])
    return _layer(x, p[0], attn_norm[0], w_in[0], mla_q_norm[0], w_q_b[0], mla_kv_norm[0],
                  w_kv_b[0], w_moba_branch[0], w_mla_branch[0], w_o[0], ffn_norm[0],
                  w_gate_up[0], w_down[0], ple_norm[0], w_ple_gate[0], w_ple_proj[0],
                  final_norm, tables)
```

```python
import jax
import jax.numpy as jnp
from jax import lax
from jax.experimental import pallas as pl
from jax.experimental.pallas import tpu as pltpu

F32 = jnp.float32
BF16 = jnp.bfloat16

D_MODEL = 1024
PLE_DIM = 256
ROPE_THETA = 10000.0
EPS = 1e-6
NEG_INF = -1e30
LOG2_E = 1.4426950408889634

HEADS = 8
HEAD_DIM = 64
MOBA_WIDTH = HEADS * HEAD_DIM
MOBA_BLOCK = 256
MOBA_TOPK = 3
MLA_Q_RANK = 256
MLA_KV_RANK = 128
MLA_ROPE_DIM = 32
MLA_QK_DIM = HEAD_DIM + MLA_ROPE_DIM
D_FF = 2816

LANES = 128
AUX_LANE = HEAD_DIM
FF_CHUNK = 256
N_FF_CHUNKS = D_FF // FF_CHUNK
assert N_FF_CHUNKS * FF_CHUNK == D_FF

ATT_TILE = 512
PROJ_TILE = ATT_TILE
POST_TILE = 512
HEADS_PER_STEP = 4
VMEM_LIMIT = 56 * 1024 * 1024


def _lane_iota(shape):
    return lax.broadcasted_iota(jnp.int32, shape, len(shape) - 1)


def _rms(x, g):
    return x * lax.rsqrt(jnp.mean(x * x, axis=-1, keepdims=True) + EPS) * g


def _dot(a, b):
    return jnp.dot(a, b, preferred_element_type=F32)


def _dot_nt(a, b):
    return lax.dot_general(a, b, (((1,), (1,)), ((), ())), preferred_element_type=F32)


def _proj_kernel(x_ref, g_ref, win_ref, wvt_ref, gq_ref, wqb_ref, gkv_ref, wkb_ref, wvbt_ref,
                 cosa_ref, sina_ref, cosb_ref, sinb_ref,
                 qa_ref, ka_ref, vat_ref, qm_ref, km_ref, vmt_ref, sga_ref, sgb_ref):
    tm = x_ref.shape[1]
    h = _rms(x_ref[0], g_ref[...]).astype(BF16)
    lane = _lane_iota((tm, LANES))
    low_half = lane < HEAD_DIM
    row = lax.broadcasted_iota(jnp.int32, (tm, LANES), 0)
    blk = pl.program_id(1) * (tm // MOBA_BLOCK) + row // MOBA_BLOCK
    blk_onehot = jnp.where(lane == AUX_LANE + blk, 1.0, 0.0).astype(F32)
    aux_rows = jnp.where(lax.broadcasted_iota(jnp.int32, (LANES - HEAD_DIM, tm), 0) == 0,
                         1.0, 0.0).astype(BF16)

    def v_tile(vt_all, hd):
        return jnp.concatenate(
            [vt_all[hd * HEAD_DIM:(hd + 1) * HEAD_DIM].astype(BF16), aux_rows], axis=0)

    cosa, sina = cosa_ref[...], sina_ref[...]
    first_a = (lane & (HEAD_DIM // 2)) == 0

    def rope_a(t):
        partner = jnp.where(first_a, pltpu.roll(t, LANES - HEAD_DIM // 2, 1),
                            pltpu.roll(t, HEAD_DIM // 2, 1))
        return t * cosa + partner * sina

    def moba_part(col0, out_ref, scale, extra):
        t_all = _dot(h, win_ref[:, col0:col0 + MOBA_WIDTH])
        for g in range(HEADS // 2):
            t = rope_a(t_all[:, g * LANES:(g + 1) * LANES])
            if scale != 1.0:
                t = t * scale
            out_ref[0, 2 * g] = (jnp.where(low_half, t, 0.0) + extra).astype(BF16)
            out_ref[0, 2 * g + 1] = (jnp.where(low_half, pltpu.roll(t, HEAD_DIM, 1), 0.0)
                                     + extra).astype(BF16)

    moba_part(0, qa_ref, HEAD_DIM ** -0.5 * LOG2_E, 0.0)
    moba_part(MOBA_WIDTH, ka_ref, 1.0, blk_onehot)
    vat = _dot_nt(wvt_ref[...], h)
    for hd in range(HEADS):
        vat_ref[0, hd, 0] = v_tile(vat, hd)

    lat_col = 2 * MOBA_WIDTH
    lat = _dot(h, win_ref[:, lat_col:lat_col + 512])
    cosb, sinb = cosb_ref[...], sinb_ref[...]
    first_b = lane < AUX_LANE + MLA_ROPE_DIM // 2

    def rope_b(t):
        partner = jnp.where(first_b, pltpu.roll(t, LANES - MLA_ROPE_DIM // 2, 1),
                            pltpu.roll(t, MLA_ROPE_DIM // 2, 1))
        return t * cosb + partner * sinb

    cq = _rms(lat[:, :MLA_Q_RANK], gq_ref[...]).astype(BF16)
    qm = _dot(cq, wqb_ref[...])
    for hd in range(HEADS):
        t = rope_b(qm[:, hd * LANES:(hd + 1) * LANES]) * (MLA_QK_DIM ** -0.5 * LOG2_E)
        qm_ref[0, hd] = t.astype(BF16)

    ckv = _rms(lat[:, MLA_Q_RANK:MLA_Q_RANK + MLA_KV_RANK], gkv_ref[...]).astype(BF16)
    kr = rope_b(pltpu.roll(lat[:, MLA_Q_RANK + MLA_KV_RANK:], AUX_LANE, 1))
    kn = _dot(ckv, wkb_ref[...])
    vmt = _dot_nt(wvbt_ref[...], ckv)
    for hd in range(HEADS):
        km_ref[0, hd] = (kn[:, hd * LANES:(hd + 1) * LANES] + kr).astype(BF16)
        vmt_ref[0, hd, 0] = v_tile(vmt, hd)

    gate_col = lat_col + 512
    sga_ref[0] = jax.nn.sigmoid(_dot(h, win_ref[:, gate_col:gate_col + D_MODEL])).astype(BF16)
    sgb_ref[0] = jax.nn.sigmoid(
        _dot(h, win_ref[:, gate_col + D_MODEL:gate_col + 2 * D_MODEL])).astype(BF16)


def _attend(q_refs, k_ref, vt_ref, o_ref, m_ref, acc_ref, buf_a, buf_b):
    i = pl.program_id(2)
    t = ATT_TILE
    n_heads = len(q_refs)
    for hh in range(n_heads):
        m_ref[hh] = jnp.full((1, t), NEG_INF, F32)
        acc_ref[hh] = jnp.zeros((LANES, t), F32)

    half = t // 2

    def scores_head(hh, j, buf, causal):
        s_ref, mx_ref = buf
        off = pl.multiple_of(j * t, t)
        if not causal:
            s = _dot_nt(k_ref[0, hh, pl.ds(off, t), :], q_refs[hh][...])
            s_ref[hh] = s
            mx_ref[hh] = jnp.max(s, axis=0, keepdims=True)
            return
        key = lax.broadcasted_iota(jnp.int32, (half, t), 0)
        qry = lax.broadcasted_iota(jnp.int32, (half, t), 1)
        top = _dot_nt(k_ref[0, hh, pl.ds(off, half), :], q_refs[hh][...])
        top = jnp.where(key <= qry, top, NEG_INF)
        bot = _dot_nt(k_ref[0, hh, pl.ds(off + half, half), :],
                      q_refs[hh][pl.ds(half, half), :])
        bot = jnp.where(lax.broadcasted_iota(jnp.int32, (half, half), 0)
                        <= lax.broadcasted_iota(jnp.int32, (half, half), 1), bot, NEG_INF)
        s_ref[hh, :half, :] = top
        s_ref[hh, half:, half:] = bot
        mx_ref[hh] = jnp.maximum(
            jnp.max(top, axis=0, keepdims=True),
            jnp.max(jnp.concatenate([jnp.full((half, half), NEG_INF, F32), bot], axis=1),
                    axis=0, keepdims=True))

    def softmax_pv_head(hh, j, buf, causal=False):
        s_ref, mx_ref = buf
        m_old = m_ref[hh]
        m_new = jnp.maximum(m_old, mx_ref[hh])
        acc = jnp.exp2(m_old - m_new) * acc_ref[hh]
        if not causal:
            p = jnp.exp2(s_ref[hh] - m_new).astype(BF16)
            acc_ref[hh] = acc + _dot(vt_ref[0, hh, j], p)
        m_ref[hh] = m_new
        if causal:
            vt = vt_ref[0, hh, j]
            p_top = jnp.exp2(s_ref[hh, :half, :] - m_new).astype(BF16)
            m_late = jnp.broadcast_to(m_new, (half, t))[:, half:]
            p_bot = jnp.exp2(s_ref[hh, half:, half:] - m_late).astype(BF16)
            pv_bot = _dot(vt[:, half:], p_bot)
            acc_ref[hh] = (acc + _dot(vt[:, :half], p_top)
                           + jnp.concatenate([jnp.zeros((LANES, half), F32), pv_bot], axis=1))

    def scores(j, buf, causal=False):
        for hh in range(n_heads):
            scores_head(hh, j, buf, causal)

    def softmax_pv(j, buf, causal=False):
        for hh in range(n_heads):
            softmax_pv_head(hh, j, buf, causal)

    def stage(j_next, buf_next, j_cur, buf_cur, causal=False):
        for hh in range(n_heads):
            scores_head(hh, j_next, buf_next, causal)
            softmax_pv_head(hh, j_cur, buf_cur)

    @pl.when(i == 0)
    def _():
        scores(0, buf_a, causal=True)
        softmax_pv(0, buf_a, causal=True)

    @pl.when(i > 0)
    def _():
        scores(0, buf_a)

        def body(jj, carry):
            j = 2 * jj
            stage(j + 1, buf_b, j, buf_a)
            stage(j + 2, buf_a, j + 1, buf_b)
            return carry

        lax.fori_loop(0, (i - 1) // 2, body, 0)

        @pl.when(i % 2 == 1)
        def _():
            stage(i, buf_b, i - 1, buf_a, causal=True)
            softmax_pv(i, buf_b, causal=True)

        @pl.when(i % 2 == 0)
        def _():
            stage(i - 1, buf_b, i - 2, buf_a)
            stage(i, buf_a, i - 1, buf_b, causal=True)
            softmax_pv(i, buf_a, causal=True)

    outs = []
    for hh in range(n_heads):
        acc = acc_ref[hh]
        outs.append(acc[:HEAD_DIM] / acc[AUX_LANE:AUX_LANE + 1])
    o_ref[0] = jnp.concatenate(outs, axis=0).T.astype(o_ref.dtype)


def _mla_kernel(q_ref, k_ref, vt_ref, o_ref, m_ref, acc_ref, sa_ref, sb_ref, mxa_ref, mxb_ref):
    _attend([q_ref.at[0, hh] for hh in range(HEADS_PER_STEP)], k_ref, vt_ref, o_ref,
            m_ref, acc_ref, (sa_ref, mxa_ref), (sb_ref, mxb_ref))


def _moba_kernel(q_ref, k_ref, vt_ref, o_ref, m_ref, acc_ref, sa_ref, sb_ref, mxa_ref, mxb_ref,
                 kmean_ref, qaug_ref):
    i = pl.program_id(2)
    t = ATT_TILE
    n_blocks = k_ref.shape[2] // MOBA_BLOCK

    @pl.when(i == 0)
    def _():
        kmean_ref[...] = jnp.zeros(kmean_ref.shape, kmean_ref.dtype)
        for hh in range(HEADS_PER_STEP):
            kb = k_ref[0, hh].astype(F32).reshape(n_blocks, MOBA_BLOCK, LANES)
            kmean_ref[hh, AUX_LANE:AUX_LANE + n_blocks, :] = (
                jnp.sum(kb, axis=1) * (1.0 / MOBA_BLOCK)).astype(kmean_ref.dtype)

    blk = lax.broadcasted_iota(jnp.int32, (n_blocks, t), 0)
    blk_f = blk.astype(F32)
    qcol = lax.broadcasted_iota(jnp.int32, (n_blocks, t), 1)
    cur = i * (t // MOBA_BLOCK) + qcol // MOBA_BLOCK
    past = blk < cur
    for hh in range(HEADS_PER_STEP):
        q = q_ref[0, hh]
        gate = _dot_nt(kmean_ref[hh], q)[AUX_LANE:AUX_LANE + n_blocks]
        gate = jnp.where(past, gate, NEG_INF)
        chosen = blk == cur
        for _ in range(MOBA_TOPK):
            best = jnp.max(gate, axis=0, keepdims=True)
            first = jnp.min(jnp.where(gate == best, blk_f, float(n_blocks)), axis=0,
                            keepdims=True)
            pick = (blk_f == first) & (best > NEG_INF * 0.5)
            chosen = chosen | pick
            gate = jnp.where(pick, NEG_INF, gate)
        bias = jnp.concatenate(
            [jnp.zeros((AUX_LANE, t), F32), jnp.where(chosen, 0.0, NEG_INF),
             jnp.zeros((LANES - AUX_LANE - n_blocks, t), F32)], axis=0).T
        qaug_ref[hh] = (q.astype(F32) + bias).astype(BF16)
    _attend([qaug_ref.at[hh] for hh in range(HEADS_PER_STEP)], k_ref, vt_ref, o_ref,
            m_ref, acc_ref, (sa_ref, mxa_ref), (sb_ref, mxb_ref))


def _post_kernel(x_ref, oa_ref, ob_ref, sga_ref, sgb_ref, p_ref,
                 wa_ref, wb_ref, wo_ref, gf_ref, wgu_ref, wd_ref,
                 gp_ref, wpg_ref, wpp_ref, gn_ref, out_ref):
    x = x_ref[0]
    ya = _dot(oa_ref[0], wa_ref[...])
    yb = _dot(ob_ref[0], wb_ref[...])
    mix = sga_ref[0].astype(F32) * ya + sgb_ref[0].astype(F32) * yb
    x = x + _dot(mix.astype(BF16), wo_ref[...])

    h = _rms(x, gf_ref[...]).astype(BF16)

    def gate_up(c):
        cols = slice(c * FF_CHUNK, (c + 1) * FF_CHUNK)
        return _dot(h, wgu_ref[:, cols]), _dot(h, wgu_ref[:, D_FF + c * FF_CHUNK:
                                                          D_FF + (c + 1) * FF_CHUNK])

    g, u = gate_up(0)
    ffn = None
    for c in range(N_FF_CHUNKS):
        a = (g * jax.nn.sigmoid(g) * u).astype(BF16)
        if c + 1 < N_FF_CHUNKS:
            g, u = gate_up(c + 1)
        d = _dot(a, wd_ref[c * FF_CHUNK:(c + 1) * FF_CHUNK, :])
        ffn = d if ffn is None else ffn + d
    x = x + ffn

    h = _rms(x, gp_ref[...]).astype(BF16)
    x = x + jax.nn.sigmoid(_dot(h, wpg_ref[...])) * _dot(p_ref[0].astype(BF16), wpp_ref[...])
    out_ref[0] = _rms(x, gn_ref[...])


def _rope_tables(seq):
    pos = jnp.arange(seq, dtype=F32)[:, None]

    def cos_sin(d):
        half = d // 2
        inv_freq = 1.0 / (ROPE_THETA ** (jnp.arange(half, dtype=F32) * (2.0 / d)))
        ang = pos * inv_freq[None, :]
        return jnp.cos(ang), jnp.sin(ang)

    ca, sa = cos_sin(HEAD_DIM)
    cosa = jnp.concatenate([ca, ca, ca, ca], axis=1)
    sina = jnp.concatenate([-sa, sa, -sa, sa], axis=1)
    cb, sb = cos_sin(MLA_ROPE_DIM)
    one = jnp.ones((seq, HEAD_DIM), F32)
    cosb = jnp.concatenate([one, cb, cb, one[:, :LANES - MLA_QK_DIM]], axis=1)
    sinb = jnp.concatenate([0 * one, -sb, sb, 0 * one[:, :LANES - MLA_QK_DIM]], axis=1)
    return cosa, sina, cosb, sinb


def _pad_heads(w, per_head, lo, hi):
    k = w.shape[0]
    w = w.reshape(k, HEADS, per_head)[:, :, lo:hi]
    return jnp.pad(w, ((0, 0), (0, 0), (0, LANES - (hi - lo)))).reshape(k, HEADS * LANES)


def _const_spec(shape):
    zeros = (0,) * len(shape)
    return pl.BlockSpec(shape, lambda *_: zeros, pipeline_mode=pl.Buffered(1))


def _params(n_axes):
    return pltpu.CompilerParams(dimension_semantics=("arbitrary",) * n_axes,
                                vmem_limit_bytes=VMEM_LIMIT)


def _layer(x, p, attn_norm, w_in, mla_q_norm, w_q_b, mla_kv_norm, w_kv_b,
           w_moba_branch, w_mla_branch, w_o, ffn_norm, w_gate_up, w_down,
           ple_norm, w_ple_gate, w_ple_proj, out_norm, tables):
    B, S, _ = x.shape
    cosa, sina, cosb, sinb = tables

    v_lo, v_hi = 2 * MOBA_WIDTH, 3 * MOBA_WIDTH
    kr_end = v_hi + MLA_Q_RANK + MLA_KV_RANK + MLA_ROPE_DIM
    w_in_p = jnp.concatenate(
        [w_in[:, :v_lo], w_in[:, v_hi:kr_end],
         jnp.zeros((D_MODEL, LANES - MLA_ROPE_DIM), w_in.dtype), w_in[:, kr_end:]],
        axis=1).astype(BF16)
    w_vt_p = w_in[:, v_lo:v_hi].T.astype(BF16)
    w_qb_p = _pad_heads(w_q_b, MLA_QK_DIM, 0, MLA_QK_DIM).astype(BF16)
    w_kb_p = _pad_heads(w_kv_b, 2 * HEAD_DIM, 0, HEAD_DIM).astype(BF16)
    w_vbt_p = w_kv_b.reshape(MLA_KV_RANK, HEADS, 2 * HEAD_DIM)[:, :, HEAD_DIM:].reshape(
        MLA_KV_RANK, MOBA_WIDTH).T.astype(BF16)
    row = lambda g: g.reshape(1, -1).astype(F32)

    tm = PROJ_TILE
    head_shape = jax.ShapeDtypeStruct((B, HEADS, S, LANES), BF16)
    vt_shape = jax.ShapeDtypeStruct((B, HEADS, S // tm, LANES, tm), BF16)
    gate_shape = jax.ShapeDtypeStruct((B, S, D_MODEL), BF16)
    head_spec = pl.BlockSpec((1, HEADS, tm, LANES), lambda b, t: (b, 0, t, 0))
    vt_spec = pl.BlockSpec((1, HEADS, 1, LANES, tm), lambda b, t: (b, 0, t, 0, 0))
    tok_spec = lambda w: pl.BlockSpec((1, tm, w), lambda b, t: (b, t, 0))
    tab_spec = pl.BlockSpec((tm, LANES), lambda b, t: (t, 0))
    qa, ka, vat, qm, km, vmt, sga, sgb = pl.pallas_call(
        _proj_kernel,
        grid=(B, S // tm),
        in_specs=[tok_spec(D_MODEL), _const_spec((1, D_MODEL)), _const_spec(w_in_p.shape),
                  _const_spec(w_vt_p.shape),
                  _const_spec((1, MLA_Q_RANK)), _const_spec(w_qb_p.shape),
                  _const_spec((1, MLA_KV_RANK)), _const_spec(w_kb_p.shape),
                  _const_spec(w_vbt_p.shape),
                  tab_spec, tab_spec, tab_spec, tab_spec],
        out_specs=[head_spec, head_spec, vt_spec, head_spec, head_spec, vt_spec,
                   tok_spec(D_MODEL), tok_spec(D_MODEL)],
        out_shape=[head_shape, head_shape, vt_shape, head_shape, head_shape, vt_shape,
                   gate_shape, gate_shape],
        compiler_params=_params(2),
        name="proj",
    )(x, row(attn_norm), w_in_p, w_vt_p, row(mla_q_norm), w_qb_p, row(mla_kv_norm), w_kb_p,
      w_vbt_p, cosa, sina, cosb, sinb)

    t = ATT_TILE
    hp = HEADS_PER_STEP
    q_spec = pl.BlockSpec((1, hp, t, LANES), lambda b, g, i: (b, g, i, 0))
    k_spec = pl.BlockSpec((1, hp, S, LANES), lambda b, g, i: (b, g, 0, 0))
    vt_all = pl.BlockSpec((1, hp, S // t, LANES, t), lambda b, g, i: (b, g, 0, 0, 0))
    o_spec = pl.BlockSpec((1, t, hp * HEAD_DIM), lambda b, g, i: (b, i, g))
    att_shape = jax.ShapeDtypeStruct((B, S, MOBA_WIDTH), BF16)
    att_grid = (B, HEADS // hp, S // t)
    stats = [pltpu.VMEM((hp, 1, t), F32), pltpu.VMEM((hp, LANES, t), F32),
             pltpu.VMEM((hp, t, t), F32), pltpu.VMEM((hp, t, t), F32),
             pltpu.VMEM((hp, 1, t), F32), pltpu.VMEM((hp, 1, t), F32)]
    oa = pl.pallas_call(
        _moba_kernel, grid=att_grid, in_specs=[q_spec, k_spec, vt_all], out_specs=o_spec,
        out_shape=att_shape,
        scratch_shapes=stats + [pltpu.VMEM((hp, LANES, LANES), BF16),
                                pltpu.VMEM((hp, t, LANES), BF16)],
        compiler_params=_params(3), name="moba",
    )(qa, ka, vat)
    ob = pl.pallas_call(
        _mla_kernel, grid=att_grid, in_specs=[q_spec, k_spec, vt_all], out_specs=o_spec,
        out_shape=att_shape, scratch_shapes=stats, compiler_params=_params(3), name="mla",
    )(qm, km, vmt)

    tp = POST_TILE
    ptok = lambda w: pl.BlockSpec((1, tp, w), lambda b, t: (b, t, 0))
    weights = [w_moba_branch.astype(BF16), w_mla_branch.astype(BF16), w_o.astype(BF16),
               row(ffn_norm), w_gate_up.astype(BF16), w_down.astype(BF16),
               row(ple_norm), w_ple_gate.astype(BF16),
               w_ple_proj.astype(BF16), row(out_norm)]
    return pl.pallas_call(
        _post_kernel,
        grid=(B, S // tp),
        in_specs=[ptok(D_MODEL), ptok(MOBA_WIDTH), ptok(MOBA_WIDTH), ptok(D_MODEL), ptok(D_MODEL),
                  ptok(PLE_DIM)] + [_const_spec(w.shape) for w in weights],
        out_specs=ptok(D_MODEL),
        out_shape=jax.ShapeDtypeStruct((B, S, D_MODEL), F32),
        compiler_params=_params(2),
        name="post",
    )(x, oa, ob, sga, sgb, p, *weights)


def kernel(x, p, attn_norm, w_in, mla_q_norm, w_q_b, mla_kv_norm, w_kv_b, w_moba_branch,
           w_mla_branch, w_o, ffn_norm, w_gate_up, w_down, ple_norm, w_ple_gate, w_ple_proj,
           final_norm):
    depth = p.shape[0]
    assert depth == 1, "the final norm is fused into the (single) layer's post kernel"
    tables = _rope_tables(x.shape[1])
    return _layer(x, p[0], attn_norm[0], w_in[0], mla_q_norm[0], w_q_b[0], mla_kv_norm[0],
                  w_kv_b[0], w_moba_branch[0], w_mla_branch[0], w_o[0], ffn_norm[0],
                  w_gate_up[0], w_down[0], ple_norm[0], w_ple_gate[0], w_ple_proj[0],
                  final_norm, tables)
```

```python
import jax
import jax.numpy as jnp
import numpy as np
from jax import lax
from jax.experimental import pallas as pl
from jax.experimental.pallas import tpu as pltpu

F32 = jnp.float32
BF16 = jnp.bfloat16

D_MODEL = 1024
PLE_DIM = 256
ROPE_THETA = 10000.0
EPS = 1e-6
NEG_INF = -1e30
LOG2_E = 1.4426950408889634

HEADS = 8
HEAD_DIM = 64
MOBA_WIDTH = HEADS * HEAD_DIM
MOBA_BLOCK = 256
MOBA_TOPK = 3
MLA_Q_RANK = 256
MLA_KV_RANK = 128
MLA_ROPE_DIM = 32
MLA_QK_DIM = HEAD_DIM + MLA_ROPE_DIM
D_FF = 2816

LANES = 128
V_ROWS = HEAD_DIM + 16
AUX_LANE = HEAD_DIM
FF_CHUNK = 256
N_FF_CHUNKS = D_FF // FF_CHUNK
assert N_FF_CHUNKS * FF_CHUNK == D_FF

ATT_TILE = 512
PROJ_TILE = ATT_TILE
POST_TILE = 512
HEADS_PER_STEP = 4
VMEM_LIMIT = 56 * 1024 * 1024


def _lane_iota(shape):
    return lax.broadcasted_iota(jnp.int32, shape, len(shape) - 1)


def _rms(x, g):
    return x * lax.rsqrt(jnp.mean(x * x, axis=-1, keepdims=True) + EPS) * g


def _dot(a, b):
    return jnp.dot(a, b, preferred_element_type=F32)


def _dot_nt(a, b):
    return lax.dot_general(a, b, (((1,), (1,)), ((), ())), preferred_element_type=F32)


def _proj_kernel(x_ref, g_ref, win_ref, wvt_ref, gq_ref, wqb_ref, gkv_ref, wkb_ref, wvbt_ref,
                 cosa_ref, sina_ref, cosb_ref, sinb_ref,
                 qa_ref, ka_ref, vat_ref, qm_ref, km_ref, vmt_ref, sga_ref, sgb_ref):
    tm = x_ref.shape[1]
    h = _rms(x_ref[0], g_ref[...]).astype(BF16)
    lane = _lane_iota((tm, LANES))
    low_half = lane < HEAD_DIM
    row = lax.broadcasted_iota(jnp.int32, (tm, LANES), 0)
    blk = pl.program_id(1) * (tm // MOBA_BLOCK) + row // MOBA_BLOCK
    blk_onehot = jnp.where(lane == AUX_LANE + blk, 1.0, 0.0).astype(F32)
    aux_rows = jnp.where(lax.broadcasted_iota(jnp.int32, (V_ROWS - HEAD_DIM, tm), 0) == 0,
                         1.0, 0.0).astype(BF16)

    def v_tile(vt_all, hd):
        return jnp.concatenate(
            [vt_all[hd * HEAD_DIM:(hd + 1) * HEAD_DIM].astype(BF16), aux_rows], axis=0)

    cosa, sina = cosa_ref[...], sina_ref[...]
    first_a = (lane & (HEAD_DIM // 2)) == 0

    def rope_a(t):
        partner = jnp.where(first_a, pltpu.roll(t, LANES - HEAD_DIM // 2, 1),
                            pltpu.roll(t, HEAD_DIM // 2, 1))
        return t * cosa + partner * sina

    def moba_part(col0, out_ref, scale, extra):
        t_all = _dot(h, win_ref[:, col0:col0 + MOBA_WIDTH])
        for g in range(HEADS // 2):
            t = rope_a(t_all[:, g * LANES:(g + 1) * LANES])
            if scale != 1.0:
                t = t * scale
            out_ref[0, 2 * g] = (jnp.where(low_half, t, 0.0) + extra).astype(BF16)
            out_ref[0, 2 * g + 1] = (jnp.where(low_half, pltpu.roll(t, HEAD_DIM, 1), 0.0)
                                     + extra).astype(BF16)

    moba_part(0, qa_ref, HEAD_DIM ** -0.5 * LOG2_E, 0.0)
    moba_part(MOBA_WIDTH, ka_ref, 1.0, blk_onehot)
    vat = _dot_nt(wvt_ref[...], h)
    for hd in range(HEADS):
        vat_ref[0, hd, 0] = v_tile(vat, hd)

    lat_col = 2 * MOBA_WIDTH
    lat = _dot(h, win_ref[:, lat_col:lat_col + 512])
    cosb, sinb = cosb_ref[...], sinb_ref[...]
    first_b = lane < AUX_LANE + MLA_ROPE_DIM // 2

    def rope_b(t):
        partner = jnp.where(first_b, pltpu.roll(t, LANES - MLA_ROPE_DIM // 2, 1),
                            pltpu.roll(t, MLA_ROPE_DIM // 2, 1))
        return t * cosb + partner * sinb

    cq = _rms(lat[:, :MLA_Q_RANK], gq_ref[...]).astype(BF16)
    qm = _dot(cq, wqb_ref[...])
    for hd in range(HEADS):
        t = rope_b(qm[:, hd * LANES:(hd + 1) * LANES]) * (MLA_QK_DIM ** -0.5 * LOG2_E)
        qm_ref[0, hd] = t.astype(BF16)

    ckv = _rms(lat[:, MLA_Q_RANK:MLA_Q_RANK + MLA_KV_RANK], gkv_ref[...]).astype(BF16)
    kr = rope_b(pltpu.roll(lat[:, MLA_Q_RANK + MLA_KV_RANK:], AUX_LANE, 1))
    kn = _dot(ckv, wkb_ref[...])
    vmt = _dot_nt(wvbt_ref[...], ckv)
    for hd in range(HEADS):
        km_ref[0, hd] = (kn[:, hd * LANES:(hd + 1) * LANES] + kr).astype(BF16)
        vmt_ref[0, hd, 0] = v_tile(vmt, hd)

    gate_col = lat_col + 512
    sga_ref[0] = jax.nn.sigmoid(_dot(h, win_ref[:, gate_col:gate_col + D_MODEL])).astype(BF16)
    sgb_ref[0] = jax.nn.sigmoid(
        _dot(h, win_ref[:, gate_col + D_MODEL:gate_col + 2 * D_MODEL])).astype(BF16)


def _attend(q_refs, k_ref, vt_ref, o_ref, m_ref, acc_ref, buf_a, buf_b):
    i = pl.program_id(2)
    t = ATT_TILE
    n_heads = len(q_refs)
    for hh in range(n_heads):
        m_ref[hh] = jnp.full((1, t), NEG_INF, F32)
        acc_ref[hh] = jnp.zeros((V_ROWS, t), F32)

    half = t // 2

    def scores_head(hh, j, buf, causal):
        s_ref, mx_ref = buf
        off = pl.multiple_of(j * t, t)
        if not causal:
            s = _dot_nt(k_ref[0, hh, pl.ds(off, t), :], q_refs[hh][...])
            s_ref[hh] = s
            mx_ref[hh] = jnp.max(s, axis=0, keepdims=True)
            return
        key = lax.broadcasted_iota(jnp.int32, (half, t), 0)
        qry = lax.broadcasted_iota(jnp.int32, (half, t), 1)
        top = _dot_nt(k_ref[0, hh, pl.ds(off, half), :], q_refs[hh][...])
        top = jnp.where(key <= qry, top, NEG_INF)
        bot = _dot_nt(k_ref[0, hh, pl.ds(off + half, half), :],
                      q_refs[hh][pl.ds(half, half), :])
        bot = jnp.where(lax.broadcasted_iota(jnp.int32, (half, half), 0)
                        <= lax.broadcasted_iota(jnp.int32, (half, half), 1), bot, NEG_INF)
        s_ref[hh, :half, :] = top
        s_ref[hh, half:, half:] = bot
        mx_ref[hh] = jnp.maximum(
            jnp.max(top, axis=0, keepdims=True),
            jnp.max(jnp.concatenate([jnp.full((half, half), NEG_INF, F32), bot], axis=1),
                    axis=0, keepdims=True))

    def softmax_pv_head(hh, j, buf, causal=False):
        s_ref, mx_ref = buf
        m_old = m_ref[hh]
        m_new = jnp.maximum(m_old, mx_ref[hh])
        acc = jnp.exp2(m_old - m_new) * acc_ref[hh]
        if not causal:
            p = jnp.exp2(s_ref[hh] - m_new).astype(BF16)
            acc_ref[hh] = acc + _dot(vt_ref[0, hh, j], p)
        m_ref[hh] = m_new
        if causal:
            vt = vt_ref[0, hh, j]
            p_top = jnp.exp2(s_ref[hh, :half, :] - m_new).astype(BF16)
            m_late = jnp.broadcast_to(m_new, (half, t))[:, half:]
            p_bot = jnp.exp2(s_ref[hh, half:, half:] - m_late).astype(BF16)
            pv_bot = _dot(vt[:, half:], p_bot)
            acc_ref[hh] = (acc + _dot(vt[:, :half], p_top)
                           + jnp.concatenate([jnp.zeros((V_ROWS, half), F32), pv_bot], axis=1))

    def scores(j, buf, causal=False):
        for hh in range(n_heads):
            scores_head(hh, j, buf, causal)

    def softmax_pv(j, buf, causal=False):
        for hh in range(n_heads):
            softmax_pv_head(hh, j, buf, causal)

    def stage(j_next, buf_next, j_cur, buf_cur, causal=False):
        for hh in range(n_heads):
            scores_head(hh, j_next, buf_next, causal)
            softmax_pv_head(hh, j_cur, buf_cur)

    @pl.when(i == 0)
    def _():
        scores(0, buf_a, causal=True)
        softmax_pv(0, buf_a, causal=True)

    @pl.when(i > 0)
    def _():
        scores(0, buf_a)

        def body(jj, carry):
            j = 2 * jj
            stage(j + 1, buf_b, j, buf_a)
            stage(j + 2, buf_a, j + 1, buf_b)
            return carry

        lax.fori_loop(0, (i - 1) // 2, body, 0)

        @pl.when(i % 2 == 1)
        def _():
            stage(i, buf_b, i - 1, buf_a, causal=True)
            softmax_pv(i, buf_b, causal=True)

        @pl.when(i % 2 == 0)
        def _():
            stage(i - 1, buf_b, i - 2, buf_a)
            stage(i, buf_a, i - 1, buf_b, causal=True)
            softmax_pv(i, buf_a, causal=True)

    outs = []
    for hh in range(n_heads):
        acc = acc_ref[hh]
        outs.append(acc[:HEAD_DIM] / acc[AUX_LANE:AUX_LANE + 1])
    o_ref[0] = jnp.concatenate(outs, axis=0).T.astype(o_ref.dtype)


def _mla_kernel(q_ref, k_ref, vt_ref, o_ref, m_ref, acc_ref, sa_ref, sb_ref, mxa_ref, mxb_ref):
    _attend([q_ref.at[0, hh] for hh in range(HEADS_PER_STEP)], k_ref, vt_ref, o_ref,
            m_ref, acc_ref, (sa_ref, mxa_ref), (sb_ref, mxb_ref))


def _moba_kernel(q_ref, k_ref, vt_ref, o_ref, m_ref, acc_ref, sa_ref, sb_ref, mxa_ref, mxb_ref,
                 kmean_ref, qaug_ref):
    i = pl.program_id(2)
    t = ATT_TILE
    n_blocks = k_ref.shape[2] // MOBA_BLOCK

    @pl.when(i == 0)
    def _():
        kmean_ref[...] = jnp.zeros(kmean_ref.shape, kmean_ref.dtype)
        for hh in range(HEADS_PER_STEP):
            kb = k_ref[0, hh].astype(F32).reshape(n_blocks, MOBA_BLOCK, LANES)
            kmean_ref[hh, AUX_LANE:AUX_LANE + n_blocks, :] = (
                jnp.sum(kb, axis=1) * (1.0 / MOBA_BLOCK)).astype(kmean_ref.dtype)

    blk = lax.broadcasted_iota(jnp.int32, (n_blocks, t), 0)
    blk_f = blk.astype(F32)
    qcol = lax.broadcasted_iota(jnp.int32, (n_blocks, t), 1)
    cur = i * (t // MOBA_BLOCK) + qcol // MOBA_BLOCK
    past = blk < cur
    for hh in range(HEADS_PER_STEP):
        q = q_ref[0, hh]
        gate = _dot_nt(kmean_ref[hh], q)[AUX_LANE:AUX_LANE + n_blocks]
        gate = jnp.where(past, gate, NEG_INF)
        chosen = blk == cur
        for _ in range(MOBA_TOPK):
            best = jnp.max(gate, axis=0, keepdims=True)
            first = jnp.min(jnp.where(gate == best, blk_f, float(n_blocks)), axis=0,
                            keepdims=True)
            pick = (blk_f == first) & (best > NEG_INF * 0.5)
            chosen = chosen | pick
            gate = jnp.where(pick, NEG_INF, gate)
        bias = jnp.concatenate(
            [jnp.zeros((AUX_LANE, t), F32), jnp.where(chosen, 0.0, NEG_INF),
             jnp.zeros((LANES - AUX_LANE - n_blocks, t), F32)], axis=0).T
        qaug_ref[hh] = (q.astype(F32) + bias).astype(BF16)
    _attend([qaug_ref.at[hh] for hh in range(HEADS_PER_STEP)], k_ref, vt_ref, o_ref,
            m_ref, acc_ref, (sa_ref, mxa_ref), (sb_ref, mxb_ref))


def _post_kernel(x_ref, oa_ref, ob_ref, sga_ref, sgb_ref, p_ref,
                 wa_ref, wb_ref, wo_ref, gf_ref, wgu_ref, wd_ref,
                 gp_ref, wpg_ref, wpp_ref, gn_ref, out_ref):
    x = x_ref[0]
    ya = _dot(oa_ref[0], wa_ref[...])
    yb = _dot(ob_ref[0], wb_ref[...])
    mix = sga_ref[0].astype(F32) * ya + sgb_ref[0].astype(F32) * yb
    x = x + _dot(mix.astype(BF16), wo_ref[...])

    h = _rms(x, gf_ref[...]).astype(BF16)

    def gate_up(c):
        cols = slice(c * FF_CHUNK, (c + 1) * FF_CHUNK)
        return _dot(h, wgu_ref[:, cols]), _dot(h, wgu_ref[:, D_FF + c * FF_CHUNK:
                                                          D_FF + (c + 1) * FF_CHUNK])

    g, u = gate_up(0)
    ffn = None
    for c in range(N_FF_CHUNKS):
        a = (g * jax.nn.sigmoid(g) * u).astype(BF16)
        if c + 1 < N_FF_CHUNKS:
            g, u = gate_up(c + 1)
        d = _dot(a, wd_ref[c * FF_CHUNK:(c + 1) * FF_CHUNK, :])
        ffn = d if ffn is None else ffn + d
    x = x + ffn

    h = _rms(x, gp_ref[...]).astype(BF16)
    x = x + jax.nn.sigmoid(_dot(h, wpg_ref[...])) * _dot(p_ref[0].astype(BF16), wpp_ref[...])
    out_ref[0] = _rms(x, gn_ref[...])


def _rope_tables(seq):
    pos = np.arange(seq, dtype=np.float64)[:, None]

    def cos_sin(d):
        half = d // 2
        inv_freq = 1.0 / (ROPE_THETA ** (np.arange(half, dtype=np.float64) * (2.0 / d)))
        ang = pos * inv_freq[None, :]
        return np.cos(ang), np.sin(ang)

    ca, sa = cos_sin(HEAD_DIM)
    cosa = np.concatenate([ca, ca, ca, ca], axis=1)
    sina = np.concatenate([-sa, sa, -sa, sa], axis=1)
    cb, sb = cos_sin(MLA_ROPE_DIM)
    one = np.ones((seq, HEAD_DIM))
    cosb = np.concatenate([one, cb, cb, one[:, :LANES - MLA_QK_DIM]], axis=1)
    sinb = np.concatenate([0 * one, -sb, sb, 0 * one[:, :LANES - MLA_QK_DIM]], axis=1)
    return tuple(jnp.asarray(t, dtype=F32) for t in (cosa, sina, cosb, sinb))


def _pad_heads(w, per_head, lo, hi):
    k = w.shape[0]
    w = w.reshape(k, HEADS, per_head)[:, :, lo:hi]
    return jnp.pad(w, ((0, 0), (0, 0), (0, LANES - (hi - lo)))).reshape(k, HEADS * LANES)


def _const_spec(shape):
    zeros = (0,) * len(shape)
    return pl.BlockSpec(shape, lambda *_: zeros, pipeline_mode=pl.Buffered(1))


def _params(n_axes):
    return pltpu.CompilerParams(dimension_semantics=("arbitrary",) * n_axes,
                                vmem_limit_bytes=VMEM_LIMIT)


def _layer(x, p, attn_norm, w_in, mla_q_norm, w_q_b, mla_kv_norm, w_kv_b,
           w_moba_branch, w_mla_branch, w_o, ffn_norm, w_gate_up, w_down,
           ple_norm, w_ple_gate, w_ple_proj, out_norm, tables):
    B, S, _ = x.shape
    cosa, sina, cosb, sinb = tables

    v_lo, v_hi = 2 * MOBA_WIDTH, 3 * MOBA_WIDTH
    kr_end = v_hi + MLA_Q_RANK + MLA_KV_RANK + MLA_ROPE_DIM
    w_in_p = jnp.concatenate(
        [w_in[:, :v_lo], w_in[:, v_hi:kr_end],
         jnp.zeros((D_MODEL, LANES - MLA_ROPE_DIM), w_in.dtype), w_in[:, kr_end:]],
        axis=1).astype(BF16)
    w_vt_p = w_in[:, v_lo:v_hi].T.astype(BF16)
    w_qb_p = _pad_heads(w_q_b, MLA_QK_DIM, 0, MLA_QK_DIM).astype(BF16)
    w_kb_p = _pad_heads(w_kv_b, 2 * HEAD_DIM, 0, HEAD_DIM).astype(BF16)
    w_vbt_p = w_kv_b.reshape(MLA_KV_RANK, HEADS, 2 * HEAD_DIM)[:, :, HEAD_DIM:].reshape(
        MLA_KV_RANK, MOBA_WIDTH).T.astype(BF16)
    row = lambda g: g.reshape(1, -1).astype(F32)

    tm = PROJ_TILE
    head_shape = jax.ShapeDtypeStruct((B, HEADS, S, LANES), BF16)
    vt_shape = jax.ShapeDtypeStruct((B, HEADS, S // tm, V_ROWS, tm), BF16)
    gate_shape = jax.ShapeDtypeStruct((B, S, D_MODEL), BF16)
    head_spec = pl.BlockSpec((1, HEADS, tm, LANES), lambda b, t: (b, 0, t, 0))
    vt_spec = pl.BlockSpec((1, HEADS, 1, V_ROWS, tm), lambda b, t: (b, 0, t, 0, 0))
    tok_spec = lambda w: pl.BlockSpec((1, tm, w), lambda b, t: (b, t, 0))
    tab_spec = pl.BlockSpec((tm, LANES), lambda b, t: (t, 0))
    qa, ka, vat, qm, km, vmt, sga, sgb = pl.pallas_call(
        _proj_kernel,
        grid=(B, S // tm),
        in_specs=[tok_spec(D_MODEL), _const_spec((1, D_MODEL)), _const_spec(w_in_p.shape),
                  _const_spec(w_vt_p.shape),
                  _const_spec((1, MLA_Q_RANK)), _const_spec(w_qb_p.shape),
                  _const_spec((1, MLA_KV_RANK)), _const_spec(w_kb_p.shape),
                  _const_spec(w_vbt_p.shape),
                  tab_spec, tab_spec, tab_spec, tab_spec],
        out_specs=[head_spec, head_spec, vt_spec, head_spec, head_spec, vt_spec,
                   tok_spec(D_MODEL), tok_spec(D_MODEL)],
        out_shape=[head_shape, head_shape, vt_shape, head_shape, head_shape, vt_shape,
                   gate_shape, gate_shape],
        compiler_params=_params(2),
        name="proj",
    )(x, row(attn_norm), w_in_p, w_vt_p, row(mla_q_norm), w_qb_p, row(mla_kv_norm), w_kb_p,
      w_vbt_p, cosa, sina, cosb, sinb)

    t = ATT_TILE
    hp = HEADS_PER_STEP
    q_spec = pl.BlockSpec((1, hp, t, LANES), lambda b, g, i: (b, g, i, 0))
    k_spec = pl.BlockSpec((1, hp, S, LANES), lambda b, g, i: (b, g, 0, 0))
    vt_all = pl.BlockSpec((1, hp, S // t, V_ROWS, t), lambda b, g, i: (b, g, 0, 0, 0))
    o_spec = pl.BlockSpec((1, t, hp * HEAD_DIM), lambda b, g, i: (b, i, g))
    att_shape = jax.ShapeDtypeStruct((B, S, MOBA_WIDTH), BF16)
    att_grid = (B, HEADS // hp, S // t)
    stats = [pltpu.VMEM((hp, 1, t), F32), pltpu.VMEM((hp, V_ROWS, t), F32),
             pltpu.VMEM((hp, t, t), F32), pltpu.VMEM((hp, t, t), F32),
             pltpu.VMEM((hp, 1, t), F32), pltpu.VMEM((hp, 1, t), F32)]
    oa = pl.pallas_call(
        _moba_kernel, grid=att_grid, in_specs=[q_spec, k_spec, vt_all], out_specs=o_spec,
        out_shape=att_shape,
        scratch_shapes=stats + [pltpu.VMEM((hp, LANES, LANES), BF16),
                                pltpu.VMEM((hp, t, LANES), BF16)],
        compiler_params=_params(3), name="moba",
    )(qa, ka, vat)
    ob = pl.pallas_call(
        _mla_kernel, grid=att_grid, in_specs=[q_spec, k_spec, vt_all], out_specs=o_spec,
        out_shape=att_shape, scratch_shapes=stats, compiler_params=_params(3), name="mla",
    )(qm, km, vmt)

    tp = POST_TILE
    ptok = lambda w: pl.BlockSpec((1, tp, w), lambda b, t: (b, t, 0))
    weights = [w_moba_branch.astype(BF16), w_mla_branch.astype(BF16), w_o.astype(BF16),
               row(ffn_norm), w_gate_up.astype(BF16), w_down.astype(BF16),
               row(ple_norm), w_ple_gate.astype(BF16),
               w_ple_proj.astype(BF16), row(out_norm)]
    return pl.pallas_call(
        _post_kernel,
        grid=(B, S // tp),
        in_specs=[ptok(D_MODEL), ptok(MOBA_WIDTH), ptok(MOBA_WIDTH), ptok(D_MODEL), ptok(D_MODEL),
                  ptok(PLE_DIM)] + [_const_spec(w.shape) for w in weights],
        out_specs=ptok(D_MODEL),
        out_shape=jax.ShapeDtypeStruct((B, S, D_MODEL), F32),
        compiler_params=_params(2),
        name="post",
    )(x, oa, ob, sga, sgb, p, *weights)


def kernel(x, p, attn_norm, w_in, mla_q_norm, w_q_b, mla_kv_norm, w_kv_b, w_moba_branch,
           w_mla_branch, w_o, ffn_norm, w_gate_up, w_down, ple_norm, w_ple_gate, w_ple_proj,
           final_norm):
    depth = p.shape[0]
    assert depth == 1, "the final norm is fused into the (single) layer's post kernel"
    tables = _rope_tables(x.shape[1])
    return _layer(x, p[0], attn_norm[0], w_in[0], mla_q_norm[0], w_q_b[0], mla_kv_norm[0],
                  w_kv_b[0], w_moba_branch[0], w_mla_branch[0], w_o[0], ffn_norm[0],
                  w_gate_up[0], w_down[0], ple_norm[0], w_ple_gate[0], w_ple_proj[0],
                  final_norm, tables)
```

```python
import jax
import jax.numpy as jnp
import numpy as np
from jax import lax
from jax.experimental import pallas as pl
from jax.experimental.pallas import tpu as pltpu

F32 = jnp.float32
BF16 = jnp.bfloat16

D_MODEL = 1024
PLE_DIM = 256
ROPE_THETA = 10000.0
EPS = 1e-6
NEG_INF = -1e30
LOG2_E = 1.4426950408889634

HEADS = 8
HEAD_DIM = 64
MOBA_WIDTH = HEADS * HEAD_DIM
MOBA_BLOCK = 256
MOBA_TOPK = 3
MLA_Q_RANK = 256
MLA_KV_RANK = 128
MLA_ROPE_DIM = 32
MLA_QK_DIM = HEAD_DIM + MLA_ROPE_DIM
D_FF = 2816

LANES = 128
V_ROWS = LANES
AUX_LANE = HEAD_DIM
FF_CHUNK = 256
N_FF_CHUNKS = D_FF // FF_CHUNK
assert N_FF_CHUNKS * FF_CHUNK == D_FF

ATT_TILE = 512
PROJ_TILE = ATT_TILE
POST_TILE = 512
HEADS_PER_STEP = 4
VMEM_LIMIT = 56 * 1024 * 1024


def _lane_iota(shape):
    return lax.broadcasted_iota(jnp.int32, shape, len(shape) - 1)


def _rms(x, g):
    return x * lax.rsqrt(jnp.mean(x * x, axis=-1, keepdims=True) + EPS) * g


def _dot(a, b):
    return jnp.dot(a, b, preferred_element_type=F32)


def _dot_nt(a, b):
    return lax.dot_general(a, b, (((1,), (1,)), ((), ())), preferred_element_type=F32)


def _proj_kernel(x_ref, g_ref, win_ref, wvt_ref, gq_ref, wqb_ref, gkv_ref, wkb_ref, wvbt_ref,
                 cosa_ref, sina_ref, cosb_ref, sinb_ref,
                 qa_ref, ka_ref, vat_ref, qm_ref, km_ref, vmt_ref, sga_ref, sgb_ref):
    tm = x_ref.shape[1]
    h = _rms(x_ref[0], g_ref[...]).astype(BF16)
    lane = _lane_iota((tm, LANES))
    low_half = lane < HEAD_DIM
    row = lax.broadcasted_iota(jnp.int32, (tm, LANES), 0)
    blk = pl.program_id(1) * (tm // MOBA_BLOCK) + row // MOBA_BLOCK
    blk_onehot = jnp.where(lane == AUX_LANE + blk, 1.0, 0.0).astype(F32)
    aux_rows = jnp.where(lax.broadcasted_iota(jnp.int32, (V_ROWS - HEAD_DIM, tm), 0) == 0,
                         1.0, 0.0).astype(BF16)

    def v_tile(vt_all, hd):
        return jnp.concatenate(
            [vt_all[hd * HEAD_DIM:(hd + 1) * HEAD_DIM].astype(BF16), aux_rows], axis=0)

    cosa, sina = cosa_ref[...], sina_ref[...]
    first_a = (lane & (HEAD_DIM // 2)) == 0

    def rope_a(t):
        partner = jnp.where(first_a, pltpu.roll(t, LANES - HEAD_DIM // 2, 1),
                            pltpu.roll(t, HEAD_DIM // 2, 1))
        return t * cosa + partner * sina

    def moba_part(col0, out_ref, scale, extra):
        t_all = _dot(h, win_ref[:, col0:col0 + MOBA_WIDTH])
        for g in range(HEADS // 2):
            t = rope_a(t_all[:, g * LANES:(g + 1) * LANES])
            if scale != 1.0:
                t = t * scale
            out_ref[0, 2 * g] = (jnp.where(low_half, t, 0.0) + extra).astype(BF16)
            out_ref[0, 2 * g + 1] = (jnp.where(low_half, pltpu.roll(t, HEAD_DIM, 1), 0.0)
                                     + extra).astype(BF16)

    moba_part(0, qa_ref, HEAD_DIM ** -0.5 * LOG2_E, 0.0)
    moba_part(MOBA_WIDTH, ka_ref, 1.0, blk_onehot)
    vat = _dot_nt(wvt_ref[...], h)
    for hd in range(HEADS):
        vat_ref[0, hd, 0] = v_tile(vat, hd)

    lat_col = 2 * MOBA_WIDTH
    lat = _dot(h, win_ref[:, lat_col:lat_col + 512])
    cosb, sinb = cosb_ref[...], sinb_ref[...]
    first_b = lane < AUX_LANE + MLA_ROPE_DIM // 2

    def rope_b(t):
        partner = jnp.where(first_b, pltpu.roll(t, LANES - MLA_ROPE_DIM // 2, 1),
                            pltpu.roll(t, MLA_ROPE_DIM // 2, 1))
        return t * cosb + partner * sinb

    cq = _rms(lat[:, :MLA_Q_RANK], gq_ref[...]).astype(BF16)
    qm = _dot(cq, wqb_ref[...])
    for hd in range(HEADS):
        t = rope_b(qm[:, hd * LANES:(hd + 1) * LANES]) * (MLA_QK_DIM ** -0.5 * LOG2_E)
        qm_ref[0, hd] = t.astype(BF16)

    ckv = _rms(lat[:, MLA_Q_RANK:MLA_Q_RANK + MLA_KV_RANK], gkv_ref[...]).astype(BF16)
    kr = rope_b(pltpu.roll(lat[:, MLA_Q_RANK + MLA_KV_RANK:], AUX_LANE, 1))
    kn = _dot(ckv, wkb_ref[...])
    vmt = _dot_nt(wvbt_ref[...], ckv)
    for hd in range(HEADS):
        km_ref[0, hd] = (kn[:, hd * LANES:(hd + 1) * LANES] + kr).astype(BF16)
        vmt_ref[0, hd, 0] = v_tile(vmt, hd)

    gate_col = lat_col + 512
    sga_ref[0] = jax.nn.sigmoid(_dot(h, win_ref[:, gate_col:gate_col + D_MODEL])).astype(BF16)
    sgb_ref[0] = jax.nn.sigmoid(
        _dot(h, win_ref[:, gate_col + D_MODEL:gate_col + 2 * D_MODEL])).astype(BF16)


def _attend(q_refs, k_ref, vt_ref, o_ref, m_ref, acc_ref, buf_a, buf_b):
    i = pl.program_id(2)
    t = ATT_TILE
    n_heads = len(q_refs)
    for hh in range(n_heads):
        m_ref[hh] = jnp.full((1, t), NEG_INF, F32)
        acc_ref[hh] = jnp.zeros((V_ROWS, t), F32)

    half = t // 2

    def scores_head(hh, j, buf, causal):
        s_ref, mx_ref = buf
        off = pl.multiple_of(j * t, t)
        if not causal:
            s = _dot_nt(k_ref[0, hh, pl.ds(off, t), :], q_refs[hh][...])
            s_ref[hh] = s
            mx_ref[hh] = jnp.max(s, axis=0, keepdims=True)
            return
        key = lax.broadcasted_iota(jnp.int32, (half, t), 0)
        qry = lax.broadcasted_iota(jnp.int32, (half, t), 1)
        top = _dot_nt(k_ref[0, hh, pl.ds(off, half), :], q_refs[hh][...])
        top = jnp.where(key <= qry, top, NEG_INF)
        bot = _dot_nt(k_ref[0, hh, pl.ds(off + half, half), :],
                      q_refs[hh][pl.ds(half, half), :])
        bot = jnp.where(lax.broadcasted_iota(jnp.int32, (half, half), 0)
                        <= lax.broadcasted_iota(jnp.int32, (half, half), 1), bot, NEG_INF)
        s_ref[hh, :half, :] = top
        s_ref[hh, half:, half:] = bot
        mx_ref[hh] = jnp.maximum(
            jnp.max(top, axis=0, keepdims=True),
            jnp.max(jnp.concatenate([jnp.full((half, half), NEG_INF, F32), bot], axis=1),
                    axis=0, keepdims=True))

    def softmax_pv_head(hh, j, buf, causal=False):
        s_ref, mx_ref = buf
        m_old = m_ref[hh]
        m_new = jnp.maximum(m_old, mx_ref[hh])
        acc = jnp.exp2(m_old - m_new) * acc_ref[hh]
        if not causal:
            p = jnp.exp2(s_ref[hh] - m_new).astype(BF16)
            acc_ref[hh] = acc + _dot(vt_ref[0, hh, j], p)
        m_ref[hh] = m_new
        if causal:
            vt = vt_ref[0, hh, j]
            p_top = jnp.exp2(s_ref[hh, :half, :] - m_new).astype(BF16)
            m_late = jnp.broadcast_to(m_new, (half, t))[:, half:]
            p_bot = jnp.exp2(s_ref[hh, half:, half:] - m_late).astype(BF16)
            pv_bot = _dot(vt[:, half:], p_bot)
            acc_ref[hh] = (acc + _dot(vt[:, :half], p_top)
                           + jnp.concatenate([jnp.zeros((V_ROWS, half), F32), pv_bot], axis=1))

    def scores(j, buf, causal=False):
        for hh in range(n_heads):
            scores_head(hh, j, buf, causal)

    def softmax_pv(j, buf, causal=False):
        for hh in range(n_heads):
            softmax_pv_head(hh, j, buf, causal)

    def stage(j_next, buf_next, j_cur, buf_cur, causal=False):
        for hh in range(n_heads):
            scores_head(hh, j_next, buf_next, causal)
            softmax_pv_head(hh, j_cur, buf_cur)

    @pl.when(i == 0)
    def _():
        scores(0, buf_a, causal=True)
        softmax_pv(0, buf_a, causal=True)

    @pl.when(i > 0)
    def _():
        scores(0, buf_a)

        def body(jj, carry):
            j = 2 * jj
            stage(j + 1, buf_b, j, buf_a)
            stage(j + 2, buf_a, j + 1, buf_b)
            return carry

        lax.fori_loop(0, (i - 1) // 2, body, 0)

        @pl.when(i % 2 == 1)
        def _():
            stage(i, buf_b, i - 1, buf_a, causal=True)
            softmax_pv(i, buf_b, causal=True)

        @pl.when(i % 2 == 0)
        def _():
            stage(i - 1, buf_b, i - 2, buf_a)
            stage(i, buf_a, i - 1, buf_b, causal=True)
            softmax_pv(i, buf_a, causal=True)

    outs = []
    for hh in range(n_heads):
        acc = acc_ref[hh]
        outs.append(acc[:HEAD_DIM] / acc[AUX_LANE:AUX_LANE + 1])
    o_ref[0] = jnp.concatenate(outs, axis=0).T.astype(o_ref.dtype)


def _mla_kernel(q_ref, k_ref, vt_ref, o_ref, m_ref, acc_ref, sa_ref, sb_ref, mxa_ref, mxb_ref):
    _attend([q_ref.at[0, hh] for hh in range(HEADS_PER_STEP)], k_ref, vt_ref, o_ref,
            m_ref, acc_ref, (sa_ref, mxa_ref), (sb_ref, mxb_ref))


def _moba_kernel(q_ref, k_ref, vt_ref, o_ref, m_ref, acc_ref, sa_ref, sb_ref, mxa_ref, mxb_ref,
                 kmean_ref, qaug_ref):
    i = pl.program_id(2)
    t = ATT_TILE
    n_blocks = k_ref.shape[2] // MOBA_BLOCK

    @pl.when(i == 0)
    def _():
        kmean_ref[...] = jnp.zeros(kmean_ref.shape, kmean_ref.dtype)
        for hh in range(HEADS_PER_STEP):
            kb = k_ref[0, hh].astype(F32).reshape(n_blocks, MOBA_BLOCK, LANES)
            kmean_ref[hh, AUX_LANE:AUX_LANE + n_blocks, :] = (
                jnp.sum(kb, axis=1) * (1.0 / MOBA_BLOCK)).astype(kmean_ref.dtype)

    blk = lax.broadcasted_iota(jnp.int32, (n_blocks, t), 0)
    blk_f = blk.astype(F32)
    qcol = lax.broadcasted_iota(jnp.int32, (n_blocks, t), 1)
    cur = i * (t // MOBA_BLOCK) + qcol // MOBA_BLOCK
    past = blk < cur
    for hh in range(HEADS_PER_STEP):
        q = q_ref[0, hh]
        gate = _dot_nt(kmean_ref[hh], q)[AUX_LANE:AUX_LANE + n_blocks]
        gate = jnp.where(past, gate, NEG_INF)
        chosen = blk == cur
        for _ in range(MOBA_TOPK):
            best = jnp.max(gate, axis=0, keepdims=True)
            first = jnp.min(jnp.where(gate == best, blk_f, float(n_blocks)), axis=0,
                            keepdims=True)
            pick = (blk_f == first) & (best > NEG_INF * 0.5)
            chosen = chosen | pick
            gate = jnp.where(pick, NEG_INF, gate)
        bias = jnp.concatenate(
            [jnp.zeros((AUX_LANE, t), F32), jnp.where(chosen, 0.0, NEG_INF),
             jnp.zeros((LANES - AUX_LANE - n_blocks, t), F32)], axis=0).T
        qaug_ref[hh] = (q.astype(F32) + bias).astype(BF16)
    _attend([qaug_ref.at[hh] for hh in range(HEADS_PER_STEP)], k_ref, vt_ref, o_ref,
            m_ref, acc_ref, (sa_ref, mxa_ref), (sb_ref, mxb_ref))


def _post_kernel(x_ref, oa_ref, ob_ref, sga_ref, sgb_ref, p_ref,
                 wa_ref, wb_ref, wo_ref, gf_ref, wgu_ref, wd_ref,
                 gp_ref, wpg_ref, wpp_ref, gn_ref, out_ref):
    x = x_ref[0]
    ya = _dot(oa_ref[0], wa_ref[...])
    yb = _dot(ob_ref[0], wb_ref[...])
    mix = sga_ref[0].astype(F32) * ya + sgb_ref[0].astype(F32) * yb
    x = x + _dot(mix.astype(BF16), wo_ref[...])

    h = _rms(x, gf_ref[...]).astype(BF16)

    def gate_up(c):
        cols = slice(c * FF_CHUNK, (c + 1) * FF_CHUNK)
        return _dot(h, wgu_ref[:, cols]), _dot(h, wgu_ref[:, D_FF + c * FF_CHUNK:
                                                          D_FF + (c + 1) * FF_CHUNK])

    g, u = gate_up(0)
    ffn = None
    for c in range(N_FF_CHUNKS):
        a = (g * jax.nn.sigmoid(g) * u).astype(BF16)
        if c + 1 < N_FF_CHUNKS:
            g, u = gate_up(c + 1)
        d = _dot(a, wd_ref[c * FF_CHUNK:(c + 1) * FF_CHUNK, :])
        ffn = d if ffn is None else ffn + d
    x = x + ffn

    h = _rms(x, gp_ref[...]).astype(BF16)
    x = x + jax.nn.sigmoid(_dot(h, wpg_ref[...])) * _dot(p_ref[0].astype(BF16), wpp_ref[...])
    out_ref[0] = _rms(x, gn_ref[...])


def _rope_tables(seq):
    pos = np.arange(seq, dtype=np.float64)[:, None]

    def cos_sin(d):
        half = d // 2
        inv_freq = 1.0 / (ROPE_THETA ** (np.arange(half, dtype=np.float64) * (2.0 / d)))
        ang = pos * inv_freq[None, :]
        return np.cos(ang), np.sin(ang)

    ca, sa = cos_sin(HEAD_DIM)
    cosa = np.concatenate([ca, ca, ca, ca], axis=1)
    sina = np.concatenate([-sa, sa, -sa, sa], axis=1)
    cb, sb = cos_sin(MLA_ROPE_DIM)
    one = np.ones((seq, HEAD_DIM))
    cosb = np.concatenate([one, cb, cb, one[:, :LANES - MLA_QK_DIM]], axis=1)
    sinb = np.concatenate([0 * one, -sb, sb, 0 * one[:, :LANES - MLA_QK_DIM]], axis=1)
    return tuple(jnp.asarray(t, dtype=F32) for t in (cosa, sina, cosb, sinb))


def _pad_heads(w, per_head, lo, hi):
    k = w.shape[0]
    w = w.reshape(k, HEADS, per_head)[:, :, lo:hi]
    return jnp.pad(w, ((0, 0), (0, 0), (0, LANES - (hi - lo)))).reshape(k, HEADS * LANES)


def _const_spec(shape):
    zeros = (0,) * len(shape)
    return pl.BlockSpec(shape, lambda *_: zeros, pipeline_mode=pl.Buffered(1))


def _params(n_axes):
    return pltpu.CompilerParams(dimension_semantics=("arbitrary",) * n_axes,
                                vmem_limit_bytes=VMEM_LIMIT)


def _layer(x, p, attn_norm, w_in, mla_q_norm, w_q_b, mla_kv_norm, w_kv_b,
           w_moba_branch, w_mla_branch, w_o, ffn_norm, w_gate_up, w_down,
           ple_norm, w_ple_gate, w_ple_proj, out_norm, tables):
    B, S, _ = x.shape
    cosa, sina, cosb, sinb = tables

    v_lo, v_hi = 2 * MOBA_WIDTH, 3 * MOBA_WIDTH
    kr_end = v_hi + MLA_Q_RANK + MLA_KV_RANK + MLA_ROPE_DIM
    w_in_p = jnp.concatenate(
        [w_in[:, :v_lo], w_in[:, v_hi:kr_end],
         jnp.zeros((D_MODEL, LANES - MLA_ROPE_DIM), w_in.dtype), w_in[:, kr_end:]],
        axis=1).astype(BF16)
    w_vt_p = w_in[:, v_lo:v_hi].T.astype(BF16)
    w_qb_p = _pad_heads(w_q_b, MLA_QK_DIM, 0, MLA_QK_DIM).astype(BF16)
    w_kb_p = _pad_heads(w_kv_b, 2 * HEAD_DIM, 0, HEAD_DIM).astype(BF16)
    w_vbt_p = w_kv_b.reshape(MLA_KV_RANK, HEADS, 2 * HEAD_DIM)[:, :, HEAD_DIM:].reshape(
        MLA_KV_RANK, MOBA_WIDTH).T.astype(BF16)
    row = lambda g: g.reshape(1, -1).astype(F32)

    tm = PROJ_TILE
    head_shape = jax.ShapeDtypeStruct((B, HEADS, S, LANES), BF16)
    vt_shape = jax.ShapeDtypeStruct((B, HEADS, S // tm, V_ROWS, tm), BF16)
    gate_shape = jax.ShapeDtypeStruct((B, S, D_MODEL), BF16)
    head_spec = pl.BlockSpec((1, HEADS, tm, LANES), lambda b, t: (b, 0, t, 0))
    vt_spec = pl.BlockSpec((1, HEADS, 1, V_ROWS, tm), lambda b, t: (b, 0, t, 0, 0))
    tok_spec = lambda w: pl.BlockSpec((1, tm, w), lambda b, t: (b, t, 0))
    tab_spec = pl.BlockSpec((tm, LANES), lambda b, t: (t, 0))
    qa, ka, vat, qm, km, vmt, sga, sgb = pl.pallas_call(
        _proj_kernel,
        grid=(B, S // tm),
        in_specs=[tok_spec(D_MODEL), _const_spec((1, D_MODEL)), _const_spec(w_in_p.shape),
                  _const_spec(w_vt_p.shape),
                  _const_spec((1, MLA_Q_RANK)), _const_spec(w_qb_p.shape),
                  _const_spec((1, MLA_KV_RANK)), _const_spec(w_kb_p.shape),
                  _const_spec(w_vbt_p.shape),
                  tab_spec, tab_spec, tab_spec, tab_spec],
        out_specs=[head_spec, head_spec, vt_spec, head_spec, head_spec, vt_spec,
                   tok_spec(D_MODEL), tok_spec(D_MODEL)],
        out_shape=[head_shape, head_shape, vt_shape, head_shape, head_shape, vt_shape,
                   gate_shape, gate_shape],
        compiler_params=_params(2),
        name="proj",
    )(x, row(attn_norm), w_in_p, w_vt_p, row(mla_q_norm), w_qb_p, row(mla_kv_norm), w_kb_p,
      w_vbt_p, cosa, sina, cosb, sinb)

    t = ATT_TILE
    hp = HEADS_PER_STEP
    q_spec = pl.BlockSpec((1, hp, t, LANES), lambda b, g, i: (b, g, i, 0))
    k_spec = pl.BlockSpec((1, hp, S, LANES), lambda b, g, i: (b, g, 0, 0))
    vt_all = pl.BlockSpec((1, hp, S // t, V_ROWS, t), lambda b, g, i: (b, g, 0, 0, 0))
    o_spec = pl.BlockSpec((1, t, hp * HEAD_DIM), lambda b, g, i: (b, i, g))
    att_shape = jax.ShapeDtypeStruct((B, S, MOBA_WIDTH), BF16)
    att_grid = (B, HEADS // hp, S // t)
    stats = [pltpu.VMEM((hp, 1, t), F32), pltpu.VMEM((hp, V_ROWS, t), F32),
             pltpu.VMEM((hp, t, t), F32), pltpu.VMEM((hp, t, t), F32),
             pltpu.VMEM((hp, 1, t), F32), pltpu.VMEM((hp, 1, t), F32)]
    oa = pl.pallas_call(
        _moba_kernel, grid=att_grid, in_specs=[q_spec, k_spec, vt_all], out_specs=o_spec,
        out_shape=att_shape,
        scratch_shapes=stats + [pltpu.VMEM((hp, LANES, LANES), BF16),
                                pltpu.VMEM((hp, t, LANES), BF16)],
        compiler_params=_params(3), name="moba",
    )(qa, ka, vat)
    ob = pl.pallas_call(
        _mla_kernel, grid=att_grid, in_specs=[q_spec, k_spec, vt_all], out_specs=o_spec,
        out_shape=att_shape, scratch_shapes=stats, compiler_params=_params(3), name="mla",
    )(qm, km, vmt)

    tp = POST_TILE
    ptok = lambda w: pl.BlockSpec((1, tp, w), lambda b, t: (b, t, 0))
    weights = [w_moba_branch.astype(BF16), w_mla_branch.astype(BF16), w_o.astype(BF16),
               row(ffn_norm), w_gate_up.astype(BF16), w_down.astype(BF16),
               row(ple_norm), w_ple_gate.astype(BF16),
               w_ple_proj.astype(BF16), row(out_norm)]
    return pl.pallas_call(
        _post_kernel,
        grid=(B, S // tp),
        in_specs=[ptok(D_MODEL), ptok(MOBA_WIDTH), ptok(MOBA_WIDTH), ptok(D_MODEL), ptok(D_MODEL),
                  ptok(PLE_DIM)] + [_const_spec(w.shape) for w in weights],
        out_specs=ptok(D_MODEL),
        out_shape=jax.ShapeDtypeStruct((B, S, D_MODEL), F32),
        compiler_params=_params(2),
        name="post",
    )(x, oa, ob, sga, sgb, p, *weights)


def kernel(x, p, attn_norm, w_in, mla_q_norm, w_q_b, mla_kv_norm, w_kv_b, w_moba_branch,
           w_mla_branch, w_o, ffn_norm, w_gate_up, w_down, ple_norm, w_ple_gate, w_ple_proj,
           final_norm):
    depth = p.shape[0]
    assert depth == 1, "the final norm is fused into the (single) layer's post kernel"
    tables = _rope_tables(x.shape[1])
    return _layer(x, p[0], attn_norm[0], w_in[0], mla_q_norm[0], w_q_b[0], mla_kv_norm[0],
                  w_kv_b[0], w_moba_branch[0], w_mla_branch[0], w_o[0], ffn_norm[0],
                  w_gate_up[0], w_down[0], ple_norm[0], w_ple_gate[0], w_ple_proj[0],
                  final_norm, tables)
```

```python
import jax
import jax.numpy as jnp
import numpy as np
from jax import lax
from jax.experimental import pallas as pl
from jax.experimental.pallas import tpu as pltpu

F32 = jnp.float32
BF16 = jnp.bfloat16

D_MODEL = 1024
PLE_DIM = 256
ROPE_THETA = 10000.0
EPS = 1e-6
NEG_INF = -1e30
LOG2_E = 1.4426950408889634

HEADS = 8
HEAD_DIM = 64
MOBA_WIDTH = HEADS * HEAD_DIM
MOBA_BLOCK = 256
MOBA_TOPK = 3
MLA_Q_RANK = 256
MLA_KV_RANK = 128
MLA_ROPE_DIM = 32
MLA_QK_DIM = HEAD_DIM + MLA_ROPE_DIM
D_FF = 2816

LANES = 128
V_ROWS = LANES
AUX_LANE = HEAD_DIM
LATENT_COLS = MLA_Q_RANK + MLA_KV_RANK + LANES
FF_CHUNK = 256
N_FF_CHUNKS = D_FF // FF_CHUNK
assert N_FF_CHUNKS * FF_CHUNK == D_FF

ATT_TILE = 512
PROJ_TILE = ATT_TILE
POST_TILE = 512
HEADS_PER_STEP = 4
VMEM_LIMIT = 56 * 1024 * 1024


def _lane_iota(shape):
    return lax.broadcasted_iota(jnp.int32, shape, len(shape) - 1)


def _rms(x, g):
    return x * lax.rsqrt(jnp.mean(x * x, axis=-1, keepdims=True) + EPS) * g


def _dot(a, b):
    return jnp.dot(a, b, preferred_element_type=F32)


def _dot_nt(a, b):
    return lax.dot_general(a, b, (((1,), (1,)), ((), ())), preferred_element_type=F32)


def _proj_kernel(x_ref, g_ref, win_ref, wvt_ref, gq_ref, wqb_ref, gkv_ref, wkb_ref, wvbt_ref,
                 cosa_ref, sina_ref, cosb_ref, sinb_ref,
                 qa_ref, ka_ref, vat_ref, qm_ref, km_ref, vmt_ref, sga_ref, sgb_ref):
    tm = x_ref.shape[1]
    h = _rms(x_ref[0], g_ref[...]).astype(BF16)
    lane = _lane_iota((tm, LANES))
    low_half = lane < HEAD_DIM
    row = lax.broadcasted_iota(jnp.int32, (tm, LANES), 0)
    blk = pl.program_id(1) * (tm // MOBA_BLOCK) + row // MOBA_BLOCK
    blk_onehot = jnp.where(lane == AUX_LANE + blk, 1.0, 0.0).astype(F32)
    aux_rows = jnp.where(lax.broadcasted_iota(jnp.int32, (V_ROWS - HEAD_DIM, tm), 0) == 0,
                         1.0, 0.0).astype(BF16)

    def v_tile(vt_all, hd):
        return jnp.concatenate(
            [vt_all[hd * HEAD_DIM:(hd + 1) * HEAD_DIM].astype(BF16), aux_rows], axis=0)

    cosa, sina = cosa_ref[...], sina_ref[...]
    first_a = (lane & (HEAD_DIM // 2)) == 0

    def rope_a(t):
        partner = jnp.where(first_a, pltpu.roll(t, LANES - HEAD_DIM // 2, 1),
                            pltpu.roll(t, HEAD_DIM // 2, 1))
        return t * cosa + partner * sina

    def moba_part(col0, out_ref, scale, extra):
        t_all = _dot(h, win_ref[:, col0:col0 + MOBA_WIDTH])
        for g in range(HEADS // 2):
            t = rope_a(t_all[:, g * LANES:(g + 1) * LANES])
            if scale != 1.0:
                t = t * scale
            out_ref[0, 2 * g] = (jnp.where(low_half, t, 0.0) + extra).astype(BF16)
            out_ref[0, 2 * g + 1] = (jnp.where(low_half, pltpu.roll(t, HEAD_DIM, 1), 0.0)
                                     + extra).astype(BF16)

    moba_part(0, qa_ref, HEAD_DIM ** -0.5 * LOG2_E, 0.0)
    moba_part(MOBA_WIDTH, ka_ref, 1.0, blk_onehot)
    vat = _dot_nt(wvt_ref[...], h)
    for hd in range(HEADS):
        vat_ref[0, hd, 0] = v_tile(vat, hd)

    lat_col = 2 * MOBA_WIDTH
    lat = _dot(h, win_ref[:, lat_col:lat_col + LATENT_COLS])
    cosb, sinb = cosb_ref[...], sinb_ref[...]
    first_b = lane < AUX_LANE + MLA_ROPE_DIM // 2

    def rope_b(t):
        partner = jnp.where(first_b, pltpu.roll(t, LANES - MLA_ROPE_DIM // 2, 1),
                            pltpu.roll(t, MLA_ROPE_DIM // 2, 1))
        return t * cosb + partner * sinb

    cq = _rms(lat[:, :MLA_Q_RANK], gq_ref[...]).astype(BF16)
    qm = _dot(cq, wqb_ref[...])
    for hd in range(HEADS):
        t = rope_b(qm[:, hd * LANES:(hd + 1) * LANES]) * (MLA_QK_DIM ** -0.5 * LOG2_E)
        qm_ref[0, hd] = t.astype(BF16)

    ckv = _rms(lat[:, MLA_Q_RANK:MLA_Q_RANK + MLA_KV_RANK], gkv_ref[...]).astype(BF16)
    kr = rope_b(pltpu.roll(lat[:, MLA_Q_RANK + MLA_KV_RANK:], AUX_LANE, 1))
    kn = _dot(ckv, wkb_ref[...])
    vmt = _dot_nt(wvbt_ref[...], ckv)
    for hd in range(HEADS):
        km_ref[0, hd] = (kn[:, hd * LANES:(hd + 1) * LANES] + kr).astype(BF16)
        vmt_ref[0, hd, 0] = v_tile(vmt, hd)

    gate_col = lat_col + LATENT_COLS
    sga_ref[0] = jax.nn.sigmoid(_dot(h, win_ref[:, gate_col:gate_col + D_MODEL])).astype(BF16)
    sgb_ref[0] = jax.nn.sigmoid(
        _dot(h, win_ref[:, gate_col + D_MODEL:gate_col + 2 * D_MODEL])).astype(BF16)


def _attend(q_refs, k_ref, vt_ref, o_ref, m_ref, acc_ref, buf_a, buf_b):
    i = pl.program_id(2)
    t = ATT_TILE
    n_heads = len(q_refs)
    for hh in range(n_heads):
        m_ref[hh] = jnp.full((1, t), NEG_INF, F32)
        acc_ref[hh] = jnp.zeros((V_ROWS, t), F32)

    half = t // 2

    def scores_head(hh, j, buf, causal):
        s_ref, mx_ref = buf
        off = pl.multiple_of(j * t, t)
        if not causal:
            s = _dot_nt(k_ref[0, hh, pl.ds(off, t), :], q_refs[hh][...])
            s_ref[hh] = s
            mx_ref[hh] = jnp.max(s, axis=0, keepdims=True)
            return
        key = lax.broadcasted_iota(jnp.int32, (half, t), 0)
        qry = lax.broadcasted_iota(jnp.int32, (half, t), 1)
        top = _dot_nt(k_ref[0, hh, pl.ds(off, half), :], q_refs[hh][...])
        top = jnp.where(key <= qry, top, NEG_INF)
        bot = _dot_nt(k_ref[0, hh, pl.ds(off + half, half), :],
                      q_refs[hh][pl.ds(half, half), :])
        bot = jnp.where(lax.broadcasted_iota(jnp.int32, (half, half), 0)
                        <= lax.broadcasted_iota(jnp.int32, (half, half), 1), bot, NEG_INF)
        s_ref[hh, :half, :] = top
        s_ref[hh, half:, half:] = bot
        mx_ref[hh] = jnp.maximum(
            jnp.max(top, axis=0, keepdims=True),
            jnp.max(jnp.concatenate([jnp.full((half, half), NEG_INF, F32), bot], axis=1),
                    axis=0, keepdims=True))

    def softmax_pv_head(hh, j, buf, causal=False):
        s_ref, mx_ref = buf
        m_old = m_ref[hh]
        m_new = jnp.maximum(m_old, mx_ref[hh])
        acc = jnp.exp2(m_old - m_new) * acc_ref[hh]
        if not causal:
            p = jnp.exp2(s_ref[hh] - m_new).astype(BF16)
            acc_ref[hh] = acc + _dot(vt_ref[0, hh, j], p)
        m_ref[hh] = m_new
        if causal:
            vt = vt_ref[0, hh, j]
            p_top = jnp.exp2(s_ref[hh, :half, :] - m_new).astype(BF16)
            m_late = jnp.broadcast_to(m_new, (half, t))[:, half:]
            p_bot = jnp.exp2(s_ref[hh, half:, half:] - m_late).astype(BF16)
            pv_bot = _dot(vt[:, half:], p_bot)
            acc_ref[hh] = (acc + _dot(vt[:, :half], p_top)
                           + jnp.concatenate([jnp.zeros((V_ROWS, half), F32), pv_bot], axis=1))

    def scores(j, buf, causal=False):
        for hh in range(n_heads):
            scores_head(hh, j, buf, causal)

    def softmax_pv(j, buf, causal=False):
        for hh in range(n_heads):
            softmax_pv_head(hh, j, buf, causal)

    def stage(j_next, buf_next, j_cur, buf_cur, causal=False):
        for hh in range(n_heads):
            scores_head(hh, j_next, buf_next, causal)
            softmax_pv_head(hh, j_cur, buf_cur)

    @pl.when(i == 0)
    def _():
        scores(0, buf_a, causal=True)
        softmax_pv(0, buf_a, causal=True)

    @pl.when(i > 0)
    def _():
        scores(0, buf_a)

        def body(jj, carry):
            j = 2 * jj
            stage(j + 1, buf_b, j, buf_a)
            stage(j + 2, buf_a, j + 1, buf_b)
            return carry

        lax.fori_loop(0, (i - 1) // 2, body, 0)

        @pl.when(i % 2 == 1)
        def _():
            stage(i, buf_b, i - 1, buf_a, causal=True)
            softmax_pv(i, buf_b, causal=True)

        @pl.when(i % 2 == 0)
        def _():
            stage(i - 1, buf_b, i - 2, buf_a)
            stage(i, buf_a, i - 1, buf_b, causal=True)
            softmax_pv(i, buf_a, causal=True)

    outs = []
    for hh in range(n_heads):
        acc = acc_ref[hh]
        outs.append(acc[:HEAD_DIM] / acc[AUX_LANE:AUX_LANE + 1])
    o_ref[0] = jnp.concatenate(outs, axis=0).T.astype(o_ref.dtype)


def _mla_kernel(q_ref, k_ref, vt_ref, o_ref, m_ref, acc_ref, sa_ref, sb_ref, mxa_ref, mxb_ref):
    _attend([q_ref.at[0, hh] for hh in range(HEADS_PER_STEP)], k_ref, vt_ref, o_ref,
            m_ref, acc_ref, (sa_ref, mxa_ref), (sb_ref, mxb_ref))


def _moba_kernel(q_ref, k_ref, vt_ref, o_ref, m_ref, acc_ref, sa_ref, sb_ref, mxa_ref, mxb_ref,
                 kmean_ref, qaug_ref):
    i = pl.program_id(2)
    t = ATT_TILE
    n_blocks = k_ref.shape[2] // MOBA_BLOCK

    @pl.when(i == 0)
    def _():
        kmean_ref[...] = jnp.zeros(kmean_ref.shape, kmean_ref.dtype)
        for hh in range(HEADS_PER_STEP):
            kb = k_ref[0, hh].astype(F32).reshape(n_blocks, MOBA_BLOCK, LANES)
            kmean_ref[hh, AUX_LANE:AUX_LANE + n_blocks, :] = (
                jnp.sum(kb, axis=1) * (1.0 / MOBA_BLOCK)).astype(kmean_ref.dtype)

    blk = lax.broadcasted_iota(jnp.int32, (n_blocks, t), 0)
    blk_f = blk.astype(F32)
    qcol = lax.broadcasted_iota(jnp.int32, (n_blocks, t), 1)
    cur = i * (t // MOBA_BLOCK) + qcol // MOBA_BLOCK
    past = blk < cur
    for hh in range(HEADS_PER_STEP):
        q = q_ref[0, hh]
        gate = _dot_nt(kmean_ref[hh], q)[AUX_LANE:AUX_LANE + n_blocks]
        gate = jnp.where(past, gate, NEG_INF)
        chosen = blk == cur
        for _ in range(MOBA_TOPK):
            best = jnp.max(gate, axis=0, keepdims=True)
            first = jnp.min(jnp.where(gate == best, blk_f, float(n_blocks)), axis=0,
                            keepdims=True)
            pick = (blk_f == first) & (best > NEG_INF * 0.5)
            chosen = chosen | pick
            gate = jnp.where(pick, NEG_INF, gate)
        bias = jnp.concatenate(
            [jnp.zeros((AUX_LANE, t), F32), jnp.where(chosen, 0.0, NEG_INF),
             jnp.zeros((LANES - AUX_LANE - n_blocks, t), F32)], axis=0).T
        qaug_ref[hh] = (q.astype(F32) + bias).astype(BF16)
    _attend([qaug_ref.at[hh] for hh in range(HEADS_PER_STEP)], k_ref, vt_ref, o_ref,
            m_ref, acc_ref, (sa_ref, mxa_ref), (sb_ref, mxb_ref))


def _post_kernel(x_ref, oa_ref, ob_ref, sga_ref, sgb_ref, p_ref,
                 wa_ref, wb_ref, wo_ref, gf_ref, wgu_ref, wd_ref,
                 gp_ref, wpg_ref, wpp_ref, gn_ref, out_ref):
    x = x_ref[0]
    ya = _dot(oa_ref[0], wa_ref[...])
    yb = _dot(ob_ref[0], wb_ref[...])
    mix = sga_ref[0].astype(F32) * ya + sgb_ref[0].astype(F32) * yb
    x = x + _dot(mix.astype(BF16), wo_ref[...])

    h = _rms(x, gf_ref[...]).astype(BF16)

    def gate_up(c):
        cols = slice(c * FF_CHUNK, (c + 1) * FF_CHUNK)
        return _dot(h, wgu_ref[:, cols]), _dot(h, wgu_ref[:, D_FF + c * FF_CHUNK:
                                                          D_FF + (c + 1) * FF_CHUNK])

    g, u = gate_up(0)
    ffn = None
    for c in range(N_FF_CHUNKS):
        a = (g * jax.nn.sigmoid(g) * u).astype(BF16)
        if c + 1 < N_FF_CHUNKS:
            g, u = gate_up(c + 1)
        d = _dot(a, wd_ref[c * FF_CHUNK:(c + 1) * FF_CHUNK, :])
        ffn = d if ffn is None else ffn + d
    x = x + ffn

    h = _rms(x, gp_ref[...]).astype(BF16)
    x = x + jax.nn.sigmoid(_dot(h, wpg_ref[...])) * _dot(p_ref[0].astype(BF16), wpp_ref[...])
    out_ref[0] = _rms(x, gn_ref[...])


def _rope_tables(seq):
    pos = np.arange(seq, dtype=np.float64)[:, None]

    def cos_sin(d):
        half = d // 2
        inv_freq = 1.0 / (ROPE_THETA ** (np.arange(half, dtype=np.float64) * (2.0 / d)))
        ang = pos * inv_freq[None, :]
        return np.cos(ang), np.sin(ang)

    ca, sa = cos_sin(HEAD_DIM)
    cosa = np.concatenate([ca, ca, ca, ca], axis=1)
    sina = np.concatenate([-sa, sa, -sa, sa], axis=1)
    cb, sb = cos_sin(MLA_ROPE_DIM)
    one = np.ones((seq, HEAD_DIM))
    cosb = np.concatenate([one, cb, cb, one[:, :LANES - MLA_QK_DIM]], axis=1)
    sinb = np.concatenate([0 * one, -sb, sb, 0 * one[:, :LANES - MLA_QK_DIM]], axis=1)
    return tuple(jnp.asarray(t, dtype=F32) for t in (cosa, sina, cosb, sinb))


def _pad_heads(w, per_head, lo, hi):
    k = w.shape[0]
    w = w.reshape(k, HEADS, per_head)[:, :, lo:hi]
    return jnp.pad(w, ((0, 0), (0, 0), (0, LANES - (hi - lo)))).reshape(k, HEADS * LANES)


def _const_spec(shape):
    zeros = (0,) * len(shape)
    return pl.BlockSpec(shape, lambda *_: zeros, pipeline_mode=pl.Buffered(1))


def _params(n_axes):
    return pltpu.CompilerParams(dimension_semantics=("arbitrary",) * n_axes,
                                vmem_limit_bytes=VMEM_LIMIT)


def _layer(x, p, attn_norm, w_in, mla_q_norm, w_q_b, mla_kv_norm, w_kv_b,
           w_moba_branch, w_mla_branch, w_o, ffn_norm, w_gate_up, w_down,
           ple_norm, w_ple_gate, w_ple_proj, out_norm, tables):
    B, S, _ = x.shape
    cosa, sina, cosb, sinb = tables

    v_lo, v_hi = 2 * MOBA_WIDTH, 3 * MOBA_WIDTH
    kr_end = v_hi + MLA_Q_RANK + MLA_KV_RANK + MLA_ROPE_DIM
    w_in_p = jnp.concatenate(
        [w_in[:, :v_lo], w_in[:, v_hi:kr_end],
         jnp.zeros((D_MODEL, LANES - MLA_ROPE_DIM), w_in.dtype), w_in[:, kr_end:]],
        axis=1).astype(BF16)
    w_vt_p = w_in[:, v_lo:v_hi].T.astype(BF16)
    w_qb_p = _pad_heads(w_q_b, MLA_QK_DIM, 0, MLA_QK_DIM).astype(BF16)
    w_kb_p = _pad_heads(w_kv_b, 2 * HEAD_DIM, 0, HEAD_DIM).astype(BF16)
    w_vbt_p = w_kv_b.reshape(MLA_KV_RANK, HEADS, 2 * HEAD_DIM)[:, :, HEAD_DIM:].reshape(
        MLA_KV_RANK, MOBA_WIDTH).T.astype(BF16)
    row = lambda g: g.reshape(1, -1).astype(F32)

    tm = PROJ_TILE
    head_shape = jax.ShapeDtypeStruct((B, HEADS, S, LANES), BF16)
    vt_shape = jax.ShapeDtypeStruct((B, HEADS, S // tm, V_ROWS, tm), BF16)
    gate_shape = jax.ShapeDtypeStruct((B, S, D_MODEL), BF16)
    head_spec = pl.BlockSpec((1, HEADS, tm, LANES), lambda b, t: (b, 0, t, 0))
    vt_spec = pl.BlockSpec((1, HEADS, 1, V_ROWS, tm), lambda b, t: (b, 0, t, 0, 0))
    tok_spec = lambda w: pl.BlockSpec((1, tm, w), lambda b, t: (b, t, 0))
    tab_spec = pl.BlockSpec((tm, LANES), lambda b, t: (t, 0))
    qa, ka, vat, qm, km, vmt, sga, sgb = pl.pallas_call(
        _proj_kernel,
        grid=(B, S // tm),
        in_specs=[tok_spec(D_MODEL), _const_spec((1, D_MODEL)), _const_spec(w_in_p.shape),
                  _const_spec(w_vt_p.shape),
                  _const_spec((1, MLA_Q_RANK)), _const_spec(w_qb_p.shape),
                  _const_spec((1, MLA_KV_RANK)), _const_spec(w_kb_p.shape),
                  _const_spec(w_vbt_p.shape),
                  tab_spec, tab_spec, tab_spec, tab_spec],
        out_specs=[head_spec, head_spec, vt_spec, head_spec, head_spec, vt_spec,
                   tok_spec(D_MODEL), tok_spec(D_MODEL)],
        out_shape=[head_shape, head_shape, vt_shape, head_shape, head_shape, vt_shape,
                   gate_shape, gate_shape],
        compiler_params=_params(2),
        name="proj",
    )(x, row(attn_norm), w_in_p, w_vt_p, row(mla_q_norm), w_qb_p, row(mla_kv_norm), w_kb_p,
      w_vbt_p, cosa, sina, cosb, sinb)

    t = ATT_TILE
    hp = HEADS_PER_STEP
    q_spec = pl.BlockSpec((1, hp, t, LANES), lambda b, g, i: (b, g, i, 0))
    k_spec = pl.BlockSpec((1, hp, S, LANES), lambda b, g, i: (b, g, 0, 0))
    vt_all = pl.BlockSpec((1, hp, S // t, V_ROWS, t), lambda b, g, i: (b, g, 0, 0, 0))
    o_spec = pl.BlockSpec((1, t, hp * HEAD_DIM), lambda b, g, i: (b, i, g))
    att_shape = jax.ShapeDtypeStruct((B, S, MOBA_WIDTH), BF16)
    att_grid = (B, HEADS // hp, S // t)
    stats = [pltpu.VMEM((hp, 1, t), F32), pltpu.VMEM((hp, V_ROWS, t), F32),
             pltpu.VMEM((hp, t, t), F32), pltpu.VMEM((hp, t, t), F32),
             pltpu.VMEM((hp, 1, t), F32), pltpu.VMEM((hp, 1, t), F32)]
    oa = pl.pallas_call(
        _moba_kernel, grid=att_grid, in_specs=[q_spec, k_spec, vt_all], out_specs=o_spec,
        out_shape=att_shape,
        scratch_shapes=stats + [pltpu.VMEM((hp, LANES, LANES), BF16),
                                pltpu.VMEM((hp, t, LANES), BF16)],
        compiler_params=_params(3), name="moba",
    )(qa, ka, vat)
    ob = pl.pallas_call(
        _mla_kernel, grid=att_grid, in_specs=[q_spec, k_spec, vt_all], out_specs=o_spec,
        out_shape=att_shape, scratch_shapes=stats, compiler_params=_params(3), name="mla",
    )(qm, km, vmt)

    tp = POST_TILE
    ptok = lambda w: pl.BlockSpec((1, tp, w), lambda b, t: (b, t, 0))
    weights = [w_moba_branch.astype(BF16), w_mla_branch.astype(BF16), w_o.astype(BF16),
               row(ffn_norm), w_gate_up.astype(BF16), w_down.astype(BF16),
               row(ple_norm), w_ple_gate.astype(BF16),
               w_ple_proj.astype(BF16), row(out_norm)]
    return pl.pallas_call(
        _post_kernel,
        grid=(B, S // tp),
        in_specs=[ptok(D_MODEL), ptok(MOBA_WIDTH), ptok(MOBA_WIDTH), ptok(D_MODEL), ptok(D_MODEL),
                  ptok(PLE_DIM)] + [_const_spec(w.shape) for w in weights],
        out_specs=ptok(D_MODEL),
        out_shape=jax.ShapeDtypeStruct((B, S, D_MODEL), F32),
        compiler_params=_params(2),
        name="post",
    )(x, oa, ob, sga, sgb, p, *weights)


def kernel(x, p, attn_norm, w_in, mla_q_norm, w_q_b, mla_kv_norm, w_kv_b, w_moba_branch,
           w_mla_branch, w_o, ffn_norm, w_gate_up, w_down, ple_norm, w_ple_gate, w_ple_proj,
           final_norm):
    depth = p.shape[0]
    assert depth == 1, "the final norm is fused into the (single) layer's post kernel"
    tables = _rope_tables(x.shape[1])
    return _layer(x, p[0], attn_norm[0], w_in[0], mla_q_norm[0], w_q_b[0], mla_kv_norm[0],
                  w_kv_b[0], w_moba_branch[0], w_mla_branch[0], w_o[0], ffn_norm[0],
                  w_gate_up[0], w_down[0], ple_norm[0], w_ple_gate[0], w_ple_proj[0],
                  final_norm, tables)
```

```python
import jax
import jax.numpy as jnp
import numpy as np
from jax import lax
from jax.experimental import pallas as pl
from jax.experimental.pallas import tpu as pltpu

F32 = jnp.float32
BF16 = jnp.bfloat16

D_MODEL = 1024
PLE_DIM = 256
ROPE_THETA = 10000.0
EPS = 1e-6
NEG_INF = -1e30
LOG2_E = 1.4426950408889634

HEADS = 8
HEAD_DIM = 64
MOBA_WIDTH = HEADS * HEAD_DIM
MOBA_BLOCK = 256
MOBA_TOPK = 3
MLA_Q_RANK = 256
MLA_KV_RANK = 128
MLA_ROPE_DIM = 32
MLA_QK_DIM = HEAD_DIM + MLA_ROPE_DIM
D_FF = 2816

LANES = 128
V_ROWS = LANES
AUX_LANE = HEAD_DIM
LATENT_COLS = MLA_Q_RANK + MLA_KV_RANK + LANES
FF_CHUNK = 256
N_FF_CHUNKS = D_FF // FF_CHUNK
assert N_FF_CHUNKS * FF_CHUNK == D_FF

ATT_TILE = 512
PROJ_TILE = ATT_TILE
POST_TILE = 512
HEADS_PER_STEP = 4
VMEM_LIMIT = 56 * 1024 * 1024


def _lane_iota(shape):
    return lax.broadcasted_iota(jnp.int32, shape, len(shape) - 1)


def _rms(x, g):
    return x * lax.rsqrt(jnp.mean(x * x, axis=-1, keepdims=True) + EPS) * g


def _dot(a, b):
    return jnp.dot(a, b, preferred_element_type=F32)


def _dot_nt(a, b):
    return lax.dot_general(a, b, (((1,), (1,)), ((), ())), preferred_element_type=F32)


def _proj_kernel(x_ref, g_ref, win_ref, wvt_ref, gq_ref, wqb_ref, gkv_ref, wkb_ref, wvbt_ref,
                 cosa_ref, sina_ref, cosb_ref, sinb_ref,
                 qa_ref, ka_ref, vat_ref, qm_ref, km_ref, vmt_ref):
    tm = x_ref.shape[1]
    h = _rms(x_ref[0], g_ref[...]).astype(BF16)
    lane = _lane_iota((tm, LANES))
    low_half = lane < HEAD_DIM
    row = lax.broadcasted_iota(jnp.int32, (tm, LANES), 0)
    blk = pl.program_id(1) * (tm // MOBA_BLOCK) + row // MOBA_BLOCK
    blk_onehot = jnp.where(lane == AUX_LANE + blk, 1.0, 0.0).astype(F32)
    aux_rows = jnp.where(lax.broadcasted_iota(jnp.int32, (V_ROWS - HEAD_DIM, tm), 0) == 0,
                         1.0, 0.0).astype(BF16)

    def v_tile(vt_all, hd):
        return jnp.concatenate(
            [vt_all[hd * HEAD_DIM:(hd + 1) * HEAD_DIM].astype(BF16), aux_rows], axis=0)

    cosa, sina = cosa_ref[...], sina_ref[...]
    first_a = (lane & (HEAD_DIM // 2)) == 0

    def rope_a(t):
        partner = jnp.where(first_a, pltpu.roll(t, LANES - HEAD_DIM // 2, 1),
                            pltpu.roll(t, HEAD_DIM // 2, 1))
        return t * cosa + partner * sina

    def moba_part(col0, out_ref, scale, extra):
        t_all = _dot(h, win_ref[:, col0:col0 + MOBA_WIDTH])
        for g in range(HEADS // 2):
            t = rope_a(t_all[:, g * LANES:(g + 1) * LANES])
            if scale != 1.0:
                t = t * scale
            out_ref[0, 2 * g] = (jnp.where(low_half, t, 0.0) + extra).astype(BF16)
            out_ref[0, 2 * g + 1] = (jnp.where(low_half, pltpu.roll(t, HEAD_DIM, 1), 0.0)
                                     + extra).astype(BF16)

    moba_part(0, qa_ref, HEAD_DIM ** -0.5 * LOG2_E, 0.0)
    moba_part(MOBA_WIDTH, ka_ref, 1.0, blk_onehot)
    vat = _dot_nt(wvt_ref[...], h)
    for hd in range(HEADS):
        vat_ref[0, hd, 0] = v_tile(vat, hd)

    lat_col = 2 * MOBA_WIDTH
    lat = _dot(h, win_ref[:, lat_col:lat_col + LATENT_COLS])
    cosb, sinb = cosb_ref[...], sinb_ref[...]
    first_b = lane < AUX_LANE + MLA_ROPE_DIM // 2

    def rope_b(t):
        partner = jnp.where(first_b, pltpu.roll(t, LANES - MLA_ROPE_DIM // 2, 1),
                            pltpu.roll(t, MLA_ROPE_DIM // 2, 1))
        return t * cosb + partner * sinb

    cq = _rms(lat[:, :MLA_Q_RANK], gq_ref[...]).astype(BF16)
    qm = _dot(cq, wqb_ref[...])
    for hd in range(HEADS):
        t = rope_b(qm[:, hd * LANES:(hd + 1) * LANES]) * (MLA_QK_DIM ** -0.5 * LOG2_E)
        qm_ref[0, hd] = t.astype(BF16)

    ckv = _rms(lat[:, MLA_Q_RANK:MLA_Q_RANK + MLA_KV_RANK], gkv_ref[...]).astype(BF16)
    kr = rope_b(pltpu.roll(lat[:, MLA_Q_RANK + MLA_KV_RANK:], AUX_LANE, 1))
    kn = _dot(ckv, wkb_ref[...])
    vmt = _dot_nt(wvbt_ref[...], ckv)
    for hd in range(HEADS):
        km_ref[0, hd] = (kn[:, hd * LANES:(hd + 1) * LANES] + kr).astype(BF16)
        vmt_ref[0, hd, 0] = v_tile(vmt, hd)


def _attend(q_refs, k_ref, vt_ref, o_ref, m_ref, acc_ref, buf_a, buf_b):
    i = pl.program_id(2)
    t = ATT_TILE
    n_heads = len(q_refs)
    for hh in range(n_heads):
        m_ref[hh] = jnp.full((1, t), NEG_INF, F32)
        acc_ref[hh] = jnp.zeros((V_ROWS, t), F32)

    half = t // 2

    def scores_head(hh, j, buf, causal):
        s_ref, mx_ref = buf
        off = pl.multiple_of(j * t, t)
        if not causal:
            s = _dot_nt(k_ref[0, hh, pl.ds(off, t), :], q_refs[hh][...])
            s_ref[hh] = s
            mx_ref[hh] = jnp.max(s, axis=0, keepdims=True)
            return
        key = lax.broadcasted_iota(jnp.int32, (half, t), 0)
        qry = lax.broadcasted_iota(jnp.int32, (half, t), 1)
        top = _dot_nt(k_ref[0, hh, pl.ds(off, half), :], q_refs[hh][...])
        top = jnp.where(key <= qry, top, NEG_INF)
        bot = _dot_nt(k_ref[0, hh, pl.ds(off + half, half), :],
                      q_refs[hh][pl.ds(half, half), :])
        bot = jnp.where(lax.broadcasted_iota(jnp.int32, (half, half), 0)
                        <= lax.broadcasted_iota(jnp.int32, (half, half), 1), bot, NEG_INF)
        s_ref[hh, :half, :] = top
        s_ref[hh, half:, half:] = bot
        mx_ref[hh] = jnp.maximum(
            jnp.max(top, axis=0, keepdims=True),
            jnp.max(jnp.concatenate([jnp.full((half, half), NEG_INF, F32), bot], axis=1),
                    axis=0, keepdims=True))

    def softmax_pv_head(hh, j, buf, causal=False):
        s_ref, mx_ref = buf
        m_old = m_ref[hh]
        m_new = jnp.maximum(m_old, mx_ref[hh])
        acc = jnp.exp2(m_old - m_new) * acc_ref[hh]
        if not causal:
            p = jnp.exp2(s_ref[hh] - m_new).astype(BF16)
            acc_ref[hh] = acc + _dot(vt_ref[0, hh, j], p)
        m_ref[hh] = m_new
        if causal:
            vt = vt_ref[0, hh, j]
            p_top = jnp.exp2(s_ref[hh, :half, :] - m_new).astype(BF16)
            m_late = jnp.broadcast_to(m_new, (half, t))[:, half:]
            p_bot = jnp.exp2(s_ref[hh, half:, half:] - m_late).astype(BF16)
            pv_bot = _dot(vt[:, half:], p_bot)
            acc_ref[hh] = (acc + _dot(vt[:, :half], p_top)
                           + jnp.concatenate([jnp.zeros((V_ROWS, half), F32), pv_bot], axis=1))

    def scores(j, buf, causal=False):
        for hh in range(n_heads):
            scores_head(hh, j, buf, causal)

    def softmax_pv(j, buf, causal=False):
        for hh in range(n_heads):
            softmax_pv_head(hh, j, buf, causal)

    def stage(j_next, buf_next, j_cur, buf_cur, causal=False):
        for hh in range(n_heads):
            scores_head(hh, j_next, buf_next, causal)
            softmax_pv_head(hh, j_cur, buf_cur)

    @pl.when(i == 0)
    def _():
        scores(0, buf_a, causal=True)
        softmax_pv(0, buf_a, causal=True)

    @pl.when(i > 0)
    def _():
        scores(0, buf_a)

        def body(jj, carry):
            j = 2 * jj
            stage(j + 1, buf_b, j, buf_a)
            stage(j + 2, buf_a, j + 1, buf_b)
            return carry

        lax.fori_loop(0, (i - 1) // 2, body, 0)

        @pl.when(i % 2 == 1)
        def _():
            stage(i, buf_b, i - 1, buf_a, causal=True)
            softmax_pv(i, buf_b, causal=True)

        @pl.when(i % 2 == 0)
        def _():
            stage(i - 1, buf_b, i - 2, buf_a)
            stage(i, buf_a, i - 1, buf_b, causal=True)
            softmax_pv(i, buf_a, causal=True)

    outs = []
    for hh in range(n_heads):
        acc = acc_ref[hh]
        outs.append(acc[:HEAD_DIM] / acc[AUX_LANE:AUX_LANE + 1])
    o_ref[0] = jnp.concatenate(outs, axis=0).T.astype(o_ref.dtype)


def _mla_kernel(q_ref, k_ref, vt_ref, o_ref, m_ref, acc_ref, sa_ref, sb_ref, mxa_ref, mxb_ref):
    _attend([q_ref.at[0, hh] for hh in range(HEADS_PER_STEP)], k_ref, vt_ref, o_ref,
            m_ref, acc_ref, (sa_ref, mxa_ref), (sb_ref, mxb_ref))


def _moba_kernel(q_ref, k_ref, vt_ref, o_ref, m_ref, acc_ref, sa_ref, sb_ref, mxa_ref, mxb_ref,
                 kmean_ref, qaug_ref):
    i = pl.program_id(2)
    t = ATT_TILE
    n_blocks = k_ref.shape[2] // MOBA_BLOCK

    @pl.when(i == 0)
    def _():
        kmean_ref[...] = jnp.zeros(kmean_ref.shape, kmean_ref.dtype)
        for hh in range(HEADS_PER_STEP):
            kb = k_ref[0, hh].astype(F32).reshape(n_blocks, MOBA_BLOCK, LANES)
            kmean_ref[hh, AUX_LANE:AUX_LANE + n_blocks, :] = (
                jnp.sum(kb, axis=1) * (1.0 / MOBA_BLOCK)).astype(kmean_ref.dtype)

    blk = lax.broadcasted_iota(jnp.int32, (n_blocks, t), 0)
    blk_f = blk.astype(F32)
    qcol = lax.broadcasted_iota(jnp.int32, (n_blocks, t), 1)
    cur = i * (t // MOBA_BLOCK) + qcol // MOBA_BLOCK
    past = blk < cur
    for hh in range(HEADS_PER_STEP):
        q = q_ref[0, hh]
        gate = _dot_nt(kmean_ref[hh], q)[AUX_LANE:AUX_LANE + n_blocks]
        gate = jnp.where(past, gate, NEG_INF)
        chosen = blk == cur
        for _ in range(MOBA_TOPK):
            best = jnp.max(gate, axis=0, keepdims=True)
            first = jnp.min(jnp.where(gate == best, blk_f, float(n_blocks)), axis=0,
                            keepdims=True)
            pick = (blk_f == first) & (best > NEG_INF * 0.5)
            chosen = chosen | pick
            gate = jnp.where(pick, NEG_INF, gate)
        bias = jnp.concatenate(
            [jnp.zeros((AUX_LANE, t), F32), jnp.where(chosen, 0.0, NEG_INF),
             jnp.zeros((LANES - AUX_LANE - n_blocks, t), F32)], axis=0).T
        qaug_ref[hh] = (q.astype(F32) + bias).astype(BF16)
    _attend([qaug_ref.at[hh] for hh in range(HEADS_PER_STEP)], k_ref, vt_ref, o_ref,
            m_ref, acc_ref, (sa_ref, mxa_ref), (sb_ref, mxb_ref))


def _post_kernel(x_ref, oa_ref, ob_ref, p_ref, ga_ref, wgate_ref,
                 wa_ref, wb_ref, wo_ref, gf_ref, wgu_ref, wd_ref,
                 gp_ref, wpg_ref, wpp_ref, gn_ref, out_ref):
    x = x_ref[0]
    h0 = _rms(x, ga_ref[...]).astype(BF16)
    ya = _dot(oa_ref[0], wa_ref[...])
    yb = _dot(ob_ref[0], wb_ref[...])
    mix = (jax.nn.sigmoid(_dot(h0, wgate_ref[:, :D_MODEL])) * ya
           + jax.nn.sigmoid(_dot(h0, wgate_ref[:, D_MODEL:])) * yb)
    x = x + _dot(mix.astype(BF16), wo_ref[...])

    h = _rms(x, gf_ref[...]).astype(BF16)

    def gate_up(c):
        cols = slice(c * FF_CHUNK, (c + 1) * FF_CHUNK)
        return _dot(h, wgu_ref[:, cols]), _dot(h, wgu_ref[:, D_FF + c * FF_CHUNK:
                                                          D_FF + (c + 1) * FF_CHUNK])

    g, u = gate_up(0)
    ffn = None
    for c in range(N_FF_CHUNKS):
        a = (g * jax.nn.sigmoid(g) * u).astype(BF16)
        if c + 1 < N_FF_CHUNKS:
            g, u = gate_up(c + 1)
        d = _dot(a, wd_ref[c * FF_CHUNK:(c + 1) * FF_CHUNK, :])
        ffn = d if ffn is None else ffn + d
    x = x + ffn

    h = _rms(x, gp_ref[...]).astype(BF16)
    x = x + jax.nn.sigmoid(_dot(h, wpg_ref[...])) * _dot(p_ref[0].astype(BF16), wpp_ref[...])
    out_ref[0] = _rms(x, gn_ref[...])


def _rope_tables(seq):
    pos = np.arange(seq, dtype=np.float64)[:, None]

    def cos_sin(d):
        half = d // 2
        inv_freq = 1.0 / (ROPE_THETA ** (np.arange(half, dtype=np.float64) * (2.0 / d)))
        ang = pos * inv_freq[None, :]
        return np.cos(ang), np.sin(ang)

    ca, sa = cos_sin(HEAD_DIM)
    cosa = np.concatenate([ca, ca, ca, ca], axis=1)
    sina = np.concatenate([-sa, sa, -sa, sa], axis=1)
    cb, sb = cos_sin(MLA_ROPE_DIM)
    one = np.ones((seq, HEAD_DIM))
    cosb = np.concatenate([one, cb, cb, one[:, :LANES - MLA_QK_DIM]], axis=1)
    sinb = np.concatenate([0 * one, -sb, sb, 0 * one[:, :LANES - MLA_QK_DIM]], axis=1)
    return tuple(jnp.asarray(t, dtype=F32) for t in (cosa, sina, cosb, sinb))


def _pad_heads(w, per_head, lo, hi):
    k = w.shape[0]
    w = w.reshape(k, HEADS, per_head)[:, :, lo:hi]
    return jnp.pad(w, ((0, 0), (0, 0), (0, LANES - (hi - lo)))).reshape(k, HEADS * LANES)


def _const_spec(shape):
    zeros = (0,) * len(shape)
    return pl.BlockSpec(shape, lambda *_: zeros, pipeline_mode=pl.Buffered(1))


def _params(n_axes):
    return pltpu.CompilerParams(dimension_semantics=("arbitrary",) * n_axes,
                                vmem_limit_bytes=VMEM_LIMIT)


def _layer(x, p, attn_norm, w_in, mla_q_norm, w_q_b, mla_kv_norm, w_kv_b,
           w_moba_branch, w_mla_branch, w_o, ffn_norm, w_gate_up, w_down,
           ple_norm, w_ple_gate, w_ple_proj, out_norm, tables):
    B, S, _ = x.shape
    cosa, sina, cosb, sinb = tables

    v_lo, v_hi = 2 * MOBA_WIDTH, 3 * MOBA_WIDTH
    kr_end = v_hi + MLA_Q_RANK + MLA_KV_RANK + MLA_ROPE_DIM
    w_in_p = jnp.concatenate(
        [w_in[:, :v_lo], w_in[:, v_hi:kr_end],
         jnp.zeros((D_MODEL, LANES - MLA_ROPE_DIM), w_in.dtype)], axis=1).astype(BF16)
    w_gate = w_in[:, kr_end:].astype(BF16)
    w_vt_p = w_in[:, v_lo:v_hi].T.astype(BF16)
    w_qb_p = _pad_heads(w_q_b, MLA_QK_DIM, 0, MLA_QK_DIM).astype(BF16)
    w_kb_p = _pad_heads(w_kv_b, 2 * HEAD_DIM, 0, HEAD_DIM).astype(BF16)
    w_vbt_p = w_kv_b.reshape(MLA_KV_RANK, HEADS, 2 * HEAD_DIM)[:, :, HEAD_DIM:].reshape(
        MLA_KV_RANK, MOBA_WIDTH).T.astype(BF16)
    row = lambda g: g.reshape(1, -1).astype(F32)

    tm = PROJ_TILE
    head_shape = jax.ShapeDtypeStruct((B, HEADS, S, LANES), BF16)
    vt_shape = jax.ShapeDtypeStruct((B, HEADS, S // tm, V_ROWS, tm), BF16)
    head_spec = pl.BlockSpec((1, HEADS, tm, LANES), lambda b, t: (b, 0, t, 0))
    vt_spec = pl.BlockSpec((1, HEADS, 1, V_ROWS, tm), lambda b, t: (b, 0, t, 0, 0))
    tok_spec = lambda w: pl.BlockSpec((1, tm, w), lambda b, t: (b, t, 0))
    tab_spec = pl.BlockSpec((tm, LANES), lambda b, t: (t, 0))
    qa, ka, vat, qm, km, vmt = pl.pallas_call(
        _proj_kernel,
        grid=(B, S // tm),
        in_specs=[tok_spec(D_MODEL), _const_spec((1, D_MODEL)), _const_spec(w_in_p.shape),
                  _const_spec(w_vt_p.shape),
                  _const_spec((1, MLA_Q_RANK)), _const_spec(w_qb_p.shape),
                  _const_spec((1, MLA_KV_RANK)), _const_spec(w_kb_p.shape),
                  _const_spec(w_vbt_p.shape),
                  tab_spec, tab_spec, tab_spec, tab_spec],
        out_specs=[head_spec, head_spec, vt_spec, head_spec, head_spec, vt_spec],
        out_shape=[head_shape, head_shape, vt_shape, head_shape, head_shape, vt_shape],
        compiler_params=_params(2),
        name="proj",
    )(x, row(attn_norm), w_in_p, w_vt_p, row(mla_q_norm), w_qb_p, row(mla_kv_norm), w_kb_p,
      w_vbt_p, cosa, sina, cosb, sinb)

    t = ATT_TILE
    hp = HEADS_PER_STEP
    q_spec = pl.BlockSpec((1, hp, t, LANES), lambda b, g, i: (b, g, i, 0))
    k_spec = pl.BlockSpec((1, hp, S, LANES), lambda b, g, i: (b, g, 0, 0))
    vt_all = pl.BlockSpec((1, hp, S // t, V_ROWS, t), lambda b, g, i: (b, g, 0, 0, 0))
    o_spec = pl.BlockSpec((1, t, hp * HEAD_DIM), lambda b, g, i: (b, i, g))
    att_shape = jax.ShapeDtypeStruct((B, S, MOBA_WIDTH), BF16)
    att_grid = (B, HEADS // hp, S // t)
    stats = [pltpu.VMEM((hp, 1, t), F32), pltpu.VMEM((hp, V_ROWS, t), F32),
             pltpu.VMEM((hp, t, t), F32), pltpu.VMEM((hp, t, t), F32),
             pltpu.VMEM((hp, 1, t), F32), pltpu.VMEM((hp, 1, t), F32)]
    oa = pl.pallas_call(
        _moba_kernel, grid=att_grid, in_specs=[q_spec, k_spec, vt_all], out_specs=o_spec,
        out_shape=att_shape,
        scratch_shapes=stats + [pltpu.VMEM((hp, LANES, LANES), BF16),
                                pltpu.VMEM((hp, t, LANES), BF16)],
        compiler_params=_params(3), name="moba",
    )(qa, ka, vat)
    ob = pl.pallas_call(
        _mla_kernel, grid=att_grid, in_specs=[q_spec, k_spec, vt_all], out_specs=o_spec,
        out_shape=att_shape, scratch_shapes=stats, compiler_params=_params(3), name="mla",
    )(qm, km, vmt)

    tp = POST_TILE
    ptok = lambda w: pl.BlockSpec((1, tp, w), lambda b, t: (b, t, 0))
    weights = [row(attn_norm), w_gate,
               w_moba_branch.astype(BF16), w_mla_branch.astype(BF16), w_o.astype(BF16),
               row(ffn_norm), w_gate_up.astype(BF16), w_down.astype(BF16),
               row(ple_norm), w_ple_gate.astype(BF16),
               w_ple_proj.astype(BF16), row(out_norm)]
    return pl.pallas_call(
        _post_kernel,
        grid=(B, S // tp),
        in_specs=[ptok(D_MODEL), ptok(MOBA_WIDTH), ptok(MOBA_WIDTH), ptok(PLE_DIM)]
                 + [_const_spec(w.shape) for w in weights],
        out_specs=ptok(D_MODEL),
        out_shape=jax.ShapeDtypeStruct((B, S, D_MODEL), F32),
        compiler_params=_params(2),
        name="post",
    )(x, oa, ob, p, *weights)


def kernel(x, p, attn_norm, w_in, mla_q_norm, w_q_b, mla_kv_norm, w_kv_b, w_moba_branch,
           w_mla_branch, w_o, ffn_norm, w_gate_up, w_down, ple_norm, w_ple_gate, w_ple_proj,
           final_norm):
    depth = p.shape[0]
    assert depth == 1, "the final norm is fused into the (single) layer's post kernel"
    tables = _rope_tables(x.shape[1])
    return _layer(x, p[0], attn_norm[0], w_in[0], mla_q_norm[0], w_q_b[0], mla_kv_norm[0],
                  w_kv_b[0], w_moba_branch[0], w_mla_branch[0], w_o[0], ffn_norm[0],
                  w_gate_up[0], w_down[0], ple_norm[0], w_ple_gate[0], w_ple_proj[0],
                  final_norm, tables)
```
